```python
import jax, jax.numpy as jnp
from jax import lax
import numpy as np

D_MODEL = 2048
BATCH = 1
SEQ = 8192
DEPTH = 4

N_MIXERS = 2
N_POOL_LAYERS = (DEPTH + 1) // 2
N_ATTN_LAYERS = DEPTH // 2
EPS = 1e-6

POOL_WINDOWS = (2, 4, 8, 16)
N_POOL_GROUPS = 4
POOL_GROUP_DIM = D_MODEL // N_POOL_GROUPS
POOL_IN = 2 * D_MODEL

N_HEADS = 16
HEAD_DIM = D_MODEL // N_HEADS
ATTN_WIDTH = N_HEADS * HEAD_DIM
ROT_DIM = HEAD_DIM // 4
IDX_HEADS = 16
IDX_DIM = 64
IDX_ROT = IDX_DIM // 4
ROPE_THETA = 500000.0
TOP_K_MAX = 256
Q_BLOCK = 128
ATTN_SPLITS = (ATTN_WIDTH, 2 * ATTN_WIDTH, 3 * ATTN_WIDTH, 4 * ATTN_WIDTH,
               4 * ATTN_WIDTH + IDX_HEADS * IDX_DIM,
               4 * ATTN_WIDTH + IDX_HEADS * IDX_DIM + IDX_DIM)
ATTN_IN = 4 * ATTN_WIDTH + IDX_HEADS * IDX_DIM + IDX_DIM + IDX_HEADS

kernel_name = "hybrid_pool_dsa_adaln_trunk"


def rms_norm(x, g):
    xf = x.astype(jnp.float32)
    y = xf * lax.rsqrt(jnp.mean(xf * xf, axis=-1, keepdims=True) + EPS)
    return (y * g.astype(jnp.float32)).astype(x.dtype)


def rope_partial(x, positions, rot_dim):
    half = rot_dim // 2
    inv_freq = ROPE_THETA ** (-jnp.arange(half, dtype=jnp.float32) * 2.0 / rot_dim)
    ang = positions.astype(jnp.float32)[..., None] * inv_freq
    cos = jnp.cos(ang)[:, :, None, :]
    sin = jnp.sin(ang)[:, :, None, :]
    xf = x.astype(jnp.float32)
    x1, x2, rest = xf[..., :half], xf[..., half:rot_dim], xf[..., rot_dim:]
    out = jnp.concatenate([x1 * cos - x2 * sin, x2 * cos + x1 * sin, rest], axis=-1)
    return out.astype(x.dtype)


def pool_branch(h, w_in, w_grp, layer_scale, w_out):
    B, L, D = h.shape
    v, gate = jnp.split(h @ w_in, 2, axis=-1)
    vf = v.astype(jnp.float32)
    cs = jnp.concatenate([jnp.zeros((B, 1, D), jnp.float32),
                          jnp.cumsum(vf, axis=1)], axis=1)
    t = jnp.arange(L, dtype=jnp.int32)
    outs = []
    for gi, w in enumerate(POOL_WINDOWS):
        sl = slice(gi * POOL_GROUP_DIM, (gi + 1) * POOL_GROUP_DIM)
        lo = jnp.maximum(t + 1 - w, 0)
        window_sum = cs[:, t + 1, sl] - cs[:, lo, sl]
        count = jnp.minimum(t + 1, w).astype(jnp.float32)[None, :, None]
        pooled = (window_sum / count - vf[:, :, sl]).astype(h.dtype)
        outs.append(jnp.einsum('bsc,cd->bsd', pooled, w_grp[gi]))
    mixed = jnp.concatenate(outs, axis=-1) * layer_scale
    return (mixed * jax.nn.silu(gate)) @ w_out


def dsa_branch(h, positions, w_in, w_out):
    B, L, _ = h.shape
    proj = h @ w_in
    q, k, v, gate, qi, ki, wi = jnp.split(proj, ATTN_SPLITS, axis=-1)
    q = rope_partial(q.reshape(B, L, N_HEADS, HEAD_DIM), positions, ROT_DIM)
    k = rope_partial(k.reshape(B, L, N_HEADS, HEAD_DIM), positions, ROT_DIM)
    v = v.reshape(B, L, N_HEADS, HEAD_DIM)
    qi = rope_partial(qi.reshape(B, L, IDX_HEADS, IDX_DIM), positions, IDX_ROT)
    ki = rope_partial(ki.reshape(B, L, 1, IDX_DIM), positions, IDX_ROT)[:, :, 0]
    ki_f = ki.astype(jnp.float32)
    top_k = min(TOP_K_MAX, L // 4)
    nb = L // Q_BLOCK
    key_idx = jnp.arange(L, dtype=jnp.int32)
    idx_scale = (IDX_HEADS ** -0.5) * (IDX_DIM ** -0.5)
    attn_scale = HEAD_DIM ** -0.5

    def to_blocks(a):
        return jnp.moveaxis(a.reshape(B, nb, Q_BLOCK, *a.shape[2:]), 1, 0)

    def block_fn(args):
        qb, qib, wb, tb = args
        s = jnp.einsum('bqhd,bsd->bqhs', qib.astype(jnp.float32), ki_f)
        score = jnp.einsum('bqhs,bqh->bqs', jax.nn.relu(s),
                           wb.astype(jnp.float32)) * idx_scale
        admissible = key_idx[None, None, :] <= tb[None, :, None]
        score = jnp.where(admissible, score, -jnp.inf)
        _, sel = lax.top_k(score, top_k)
        valid = sel <= tb[None, :, None]
        k_sel = jax.vmap(lambda kk, ii: kk[ii])(k, sel)
        v_sel = jax.vmap(lambda vv, ii: vv[ii])(v, sel)
        logits = jnp.einsum('bqhd,bqkhd->bhqk', qb, k_sel).astype(jnp.float32) * attn_scale
        logits = jnp.where(valid[:, None], logits, -jnp.inf)
        p = jax.nn.softmax(logits, axis=-1).astype(v.dtype)
        return jnp.einsum('bhqk,bqkhd->bqhd', p, v_sel)

    t_blocks = key_idx.reshape(nb, Q_BLOCK)
    outs = lax.map(block_fn, (to_blocks(q), to_blocks(qi), to_blocks(wi), t_blocks))
    o = jnp.moveaxis(outs, 0, 1).reshape(B, L, ATTN_WIDTH)
    return (o * jax.nn.silu(gate)) @ w_out


def setup_inputs(seed: int = 0) -> dict:
    key = jax.random.key(seed)
    ks = jax.random.split(key, 13)
    D = D_MODEL
    x = jax.random.normal(ks[0], (BATCH, SEQ, D), jnp.float32)
    c = jax.random.normal(ks[1], (BATCH, D), jnp.float32)
    positions = jnp.broadcast_to(jnp.arange(SEQ, dtype=jnp.int32)[None, :], (BATCH, SEQ))
    norm_g = 1.0 + 0.02 * jax.random.normal(ks[2], (DEPTH, D), jnp.float32)
    mod_w = 0.5 * D ** -0.5 * jax.random.normal(ks[3], (DEPTH, D, 3 * D), jnp.float32)
    mod_b = 0.02 * jax.random.normal(ks[4], (DEPTH, 3 * D), jnp.float32)
    pool_w_in = D ** -0.5 * jax.random.normal(ks[5], (N_POOL_LAYERS, D, POOL_IN), jnp.float32)
    pool_w_grp = POOL_GROUP_DIM ** -0.5 * jax.random.normal(
        ks[6], (N_POOL_LAYERS, N_POOL_GROUPS, POOL_GROUP_DIM, POOL_GROUP_DIM), jnp.float32)
    pool_scale = 1.0 + 0.1 * jax.random.normal(ks[7], (N_POOL_LAYERS, D), jnp.float32)
    pool_w_out = D ** -0.5 * jax.random.normal(ks[8], (N_POOL_LAYERS, D, D), jnp.float32)
    attn_w_in = D ** -0.5 * jax.random.normal(ks[9], (N_ATTN_LAYERS, D, ATTN_IN), jnp.float32)
    attn_w_out = ATTN_WIDTH ** -0.5 * jax.random.normal(
        ks[10], (N_ATTN_LAYERS, ATTN_WIDTH, D), jnp.float32)
    final_g = 1.0 + 0.02 * jax.random.normal(ks[11], (D,), jnp.float32)
    return {"x": x, "c": c, "positions": positions, "norm_g": norm_g,
            "mod_w": mod_w, "mod_b": mod_b, "pool_w_in": pool_w_in,
            "pool_w_grp": pool_w_grp, "pool_scale": pool_scale,
            "pool_w_out": pool_w_out, "attn_w_in": attn_w_in,
            "attn_w_out": attn_w_out, "final_g": final_g}


def reference(x, c, positions, norm_g, mod_w, mod_b, pool_w_in, pool_w_grp,
              pool_scale, pool_w_out, attn_w_in, attn_w_out, final_g):
    cond = jax.nn.silu(c)
    for i in range(DEPTH):
        mod = cond @ mod_w[i] + mod_b[i]
        shift, scale, gate = jnp.split(mod, 3, axis=-1)
        h = rms_norm(x, norm_g[i]) * (1.0 + scale[:, None, :]) + shift[:, None, :]
        j = i // N_MIXERS
        if i % N_MIXERS == 0:
            out = pool_branch(h, pool_w_in[j], pool_w_grp[j], pool_scale[j], pool_w_out[j])
        else:
            out = dsa_branch(h, positions, attn_w_in[j], attn_w_out[j])
        x = x + gate[:, None, :] * out
    return rms_norm(x, final_g)
```

```python
import functools
import math

import jax
import jax.numpy as jnp
from jax import lax
from jax.experimental import pallas as pl
from jax.experimental.pallas import tpu as pltpu

F32 = jnp.float32
BF16 = jnp.bfloat16

D_MODEL = 2048
DEPTH = 4
EPS = 1e-6
POOL_WINDOWS = (2, 4, 8, 16)
POOL_GROUP_DIM = D_MODEL // len(POOL_WINDOWS)
POOL_HALO = 16
N_HEADS = 16
HEAD_DIM = 128
ATTN_WIDTH = N_HEADS * HEAD_DIM
ROT_DIM = HEAD_DIM // 4
IDX_HEADS = 16
IDX_DIM = 64
IDX_ROT = IDX_DIM // 4
ROPE_THETA = 500000.0
TOP_K_MAX = 256
LANES = 128
VMEM_LIMIT = 56 * 1024 * 1024

NEG_MASK = -1e30
SELECT_ALL = -1e29
MAX_BISECT = 64


def _params(sem):
    return pltpu.CompilerParams(dimension_semantics=sem, vmem_limit_bytes=VMEM_LIMIT)


def _mod_kernel(c_ref, w_ref, b_ref, o_ref, cond_ref, *, kc):
    @pl.when((pl.program_id(0) == 0) & (pl.program_id(1) == 0))
    def _():
        cc = c_ref[...]
        cond_ref[...] = jnp.broadcast_to(cc * jax.nn.sigmoid(cc), cond_ref.shape)

    tn = o_ref.shape[-1]

    def body(i, acc):
        k0 = pl.multiple_of(i * kc, kc)
        wk = w_ref[0, pl.ds(k0, kc), :]
        ck = jnp.tile(cond_ref[pl.ds(k0, kc), :], (1, tn // LANES))
        return acc + jnp.sum((wk * ck).reshape(kc // 8, 8, tn), axis=0)

    acc = lax.fori_loop(0, D_MODEL // kc, body, jnp.zeros((8, tn), F32))
    o_ref[0] = jnp.sum(acc, axis=0, keepdims=True) + b_ref[0]


def _modulation(c, mod_w, mod_b):
    depth, d, n = mod_w.shape
    tn = 512
    out = pl.pallas_call(
        functools.partial(_mod_kernel, kc=64),
        grid=(depth, n // tn),
        in_specs=[
            pl.BlockSpec((d, 1), lambda l, j: (0, 0)),
            pl.BlockSpec((1, d, tn), lambda l, j: (l, 0, j)),
            pl.BlockSpec((1, 1, tn), lambda l, j: (l, 0, j)),
        ],
        out_specs=pl.BlockSpec((1, 1, tn), lambda l, j: (l, 0, j)),
        out_shape=jax.ShapeDtypeStruct((depth, 1, n), F32),
        scratch_shapes=[pltpu.VMEM((d, LANES), F32)],
        compiler_params=_params(("arbitrary", "arbitrary")),
        name="adaln_modulation",
    )(c.reshape(d, 1), mod_w, mod_b.reshape(depth, 1, n))
    return out


def _norm_kernel(x_ref, g_ref, sc_ref, sh_ref, o_ref, *, modulate):
    x = x_ref[...]
    ms = jnp.mean(x * x, axis=-1, keepdims=True)
    y = x * lax.rsqrt(ms + EPS) * g_ref[...]
    if modulate:
        y = y * (1.0 + sc_ref[...]) + sh_ref[...]
    o_ref[...] = y.astype(o_ref.dtype)


def _norm(x, g, scale, shift, *, modulate, out_dtype, tm=256):
    s, d = x.shape
    row = pl.BlockSpec((1, d), lambda i: (0, 0))
    return pl.pallas_call(
        functools.partial(_norm_kernel, modulate=modulate),
        grid=(s // tm,),
        in_specs=[pl.BlockSpec((tm, d), lambda i: (i, 0)), row, row, row],
        out_specs=pl.BlockSpec((tm, d), lambda i: (i, 0)),
        out_shape=jax.ShapeDtypeStruct((s, d), out_dtype),
        compiler_params=_params(("arbitrary",)),
        name="rmsnorm_modulate" if modulate else "rmsnorm_final",
    )(x, g, scale, shift)


def _rope_apply(acc, tab, half):
    tn = acc.shape[1]
    reps = tn // LANES
    c = jnp.tile(tab[:, :LANES], (1, reps))
    s1 = jnp.tile(tab[:, LANES:2 * LANES], (1, reps))
    s2 = jnp.tile(tab[:, 2 * LANES:], (1, reps))
    return (acc * c + pltpu.roll(acc, half, 1) * s1 + pltpu.roll(acc, tn - half, 1) * s2)


def _mm_plain_kernel(a_ref, w_ref, o_ref):
    acc = jnp.dot(a_ref[...], w_ref[...], preferred_element_type=F32)
    o_ref[...] = acc.astype(o_ref.dtype)


def _mm_silu_kernel(a_ref, w_ref, o_ref):
    acc = jnp.dot(a_ref[...], w_ref[...], preferred_element_type=F32)
    o_ref[...] = (acc * jax.nn.sigmoid(acc)).astype(o_ref.dtype)


def _mm_rope_kernel(a_ref, w_ref, tab_ref, o_ref, *, half):
    acc = jnp.dot(a_ref[...], w_ref[...], preferred_element_type=F32)
    o_ref[...] = _rope_apply(acc, tab_ref[0], half).astype(o_ref.dtype)


def _mm_resid_kernel(a_ref, w_ref, x_ref, gm_ref, o_ref):
    acc = jnp.dot(a_ref[...], w_ref[...], preferred_element_type=F32)
    o_ref[...] = x_ref[...] + gm_ref[...] * acc


def _mm_idx_kernel(a_ref, w_ref, tab_ref, oq_ref, okw_ref, *, half):
    acc = jnp.dot(a_ref[...], w_ref[...], preferred_element_type=F32)
    nq = oq_ref.shape[1]
    groups = nq // LANES
    tab_q = tab_ref[0]
    tab_kw = tab_ref[1]
    c = jnp.concatenate([tab_q[:, :LANES]] * groups + [tab_kw[:, :LANES]], axis=1)
    s1 = jnp.concatenate([tab_q[:, LANES:2 * LANES]] * groups + [tab_kw[:, LANES:2 * LANES]], axis=1)
    s2 = jnp.concatenate([tab_q[:, 2 * LANES:]] * groups + [tab_kw[:, 2 * LANES:]], axis=1)
    tn = acc.shape[1]
    out = acc * c + pltpu.roll(acc, half, 1) * s1 + pltpu.roll(acc, tn - half, 1) * s2
    oq_ref[...] = out[:, :nq].astype(oq_ref.dtype)
    okw_ref[...] = out[:, nq:]


def _mm(a, w, *, epilogue, out_dtype, tm=512, tn=512, tab=None, tab_base=0, tab_split=None,
        half=None, x=None, gm=None, name="mm"):
    s, k = a.shape
    n = w.shape[1]
    tm = min(tm, s)
    tn = min(tn, n)
    grid = (s // tm, n // tn)
    a_spec = pl.BlockSpec((tm, k), lambda i, j: (i, 0))
    w_spec = pl.BlockSpec((k, tn), lambda i, j: (0, j))
    o_spec = pl.BlockSpec((tm, tn), lambda i, j: (i, j))
    out_shape = jax.ShapeDtypeStruct((s, n), out_dtype)
    if epilogue == "plain":
        kern, ins, specs = _mm_plain_kernel, (a, w), [a_spec, w_spec]
    elif epilogue == "silu":
        kern, ins, specs = _mm_silu_kernel, (a, w), [a_spec, w_spec]
    elif epilogue == "rope":
        kern = functools.partial(_mm_rope_kernel, half=half)
        ins = (a, w, tab)
        specs = [a_spec, w_spec,
                 pl.BlockSpec((1, tm, 3 * LANES), lambda i, j: (tab_base + j // tab_split, i, 0))]
    elif epilogue == "resid":
        kern, ins = _mm_resid_kernel, (a, w, x, gm)
        specs = [a_spec, w_spec, pl.BlockSpec((tm, tn), lambda i, j: (i, j)),
                 pl.BlockSpec((1, tn), lambda i, j: (0, j))]
    else:
        raise ValueError(epilogue)
    return pl.pallas_call(
        kern, grid=grid, in_specs=specs, out_specs=o_spec, out_shape=out_shape,
        compiler_params=_params(("arbitrary", "arbitrary")), name=name,
    )(*ins)


def _mm_idx(a, w, tab, *, tab_pair, half, tm=512):
    s, k = a.shape
    n = w.shape[1]
    nq = IDX_HEADS * IDX_DIM
    tm = min(tm, s)
    return pl.pallas_call(
        functools.partial(_mm_idx_kernel, half=half),
        grid=(s // tm,),
        in_specs=[pl.BlockSpec((tm, k), lambda i: (i, 0)),
                  pl.BlockSpec((k, n), lambda i: (0, 0)),
                  pl.BlockSpec((2, tm, 3 * LANES), lambda i: (tab_pair, i, 0))],
        out_specs=[pl.BlockSpec((tm, nq), lambda i: (i, 0)),
                   pl.BlockSpec((tm, LANES), lambda i: (i, 0))],
        out_shape=[jax.ShapeDtypeStruct((s, nq), BF16),
                   jax.ShapeDtypeStruct((s, LANES), F32)],
        compiler_params=_params(("arbitrary",)),
        name="indexer_projection",
    )(a, w, tab)


def _rope_table_kernel(pos_ref, cst_ref, o_ref):
    pos = pos_ref[...].astype(F32)
    for v in range(o_ref.shape[0]):
        cst = cst_ref[v]
        ang = pos * cst[0:1, :]
        cos = jnp.cos(ang)
        sin = jnp.sin(ang)
        first, second, base, scale = cst[1:2, :], cst[2:3, :], cst[3:4, :], cst[4:5, :]
        rot = first + second
        o_ref[v, :, 0:LANES] = (rot * cos + (1.0 - rot) * base) * scale
        o_ref[v, :, LANES:2 * LANES] = second * sin * scale
        o_ref[v, :, 2 * LANES:3 * LANES] = -(first * sin) * scale


def _rope_tables(positions, consts, tm=512):
    s = positions.shape[0]
    nv = consts.shape[0]
    tm = min(tm, s)
    return pl.pallas_call(
        _rope_table_kernel,
        grid=(s // tm,),
        in_specs=[pl.BlockSpec((tm, 1), lambda i: (i, 0)),
                  pl.BlockSpec((nv, 8, LANES), lambda i: (0, 0, 0))],
        out_specs=pl.BlockSpec((nv, tm, 3 * LANES), lambda i: (0, i, 0)),
        out_shape=jax.ShapeDtypeStruct((nv, s, 3 * LANES), F32),
        compiler_params=_params(("arbitrary",)),
        name="rope_tables",
    )(positions.reshape(s, 1), consts)


def _rope_consts(head_dim, rot_dim, scale, n_rot_lanes, base_val, base_lo, base_hi):
    half = rot_dim // 2
    inv_freq = ROPE_THETA ** (-jnp.arange(half, dtype=F32) * 2.0 / rot_dim)
    lane = jnp.arange(LANES)
    inner = lane % head_dim
    active = lane < n_rot_lanes
    first = (inner < half) & active
    second = (inner >= half) & (inner < rot_dim) & active
    invf = jnp.where(first | second, inv_freq[inner % half], 0.0)
    base = jnp.where((lane >= base_lo) & (lane < base_hi), base_val, 1.0)
    rows = [invf, first.astype(F32), second.astype(F32), base.astype(F32),
            jnp.full((LANES,), scale, F32)]
    rows += [jnp.zeros((LANES,), F32)] * (8 - len(rows))
    return jnp.stack(rows).astype(F32)


def _pool_kernel(v_ref, halo_ref, sg_ref, wg_ref, ls_ref, o_ref, *, tm):
    i = pl.program_id(0)
    keep = (i > 0).astype(F32)
    row = i * tm + lax.broadcasted_iota(jnp.int32, (tm, 1), 0)
    gd = POOL_GROUP_DIM
    for g, w in enumerate(POOL_WINDOWS):
        sl = slice(g * gd, (g + 1) * gd)
        vg = v_ref[:, sl]
        cat = jnp.concatenate([halo_ref[:, sl] * keep, vg], axis=0)
        step = 1
        while step < w:
            cat = cat + pltpu.roll(cat, step, 0)
            step *= 2
        win = cat[POOL_HALO:, :]
        cnt = jnp.minimum(row + 1, w).astype(F32)
        pooled = (win / cnt - vg).astype(BF16)
        mixed = jnp.dot(pooled, wg_ref[g], preferred_element_type=F32)
        mixed = mixed * ls_ref[:, sl] * sg_ref[:, sl].astype(F32)
        o_ref[:, sl] = mixed.astype(o_ref.dtype)


def _pool_core(v, sgate, w_grp, layer_scale, tm=256):
    s, d = v.shape
    tm = min(tm, s)
    per = tm // POOL_HALO
    return pl.pallas_call(
        functools.partial(_pool_kernel, tm=tm),
        grid=(s // tm,),
        in_specs=[pl.BlockSpec((tm, d), lambda i: (i, 0)),
                  pl.BlockSpec((POOL_HALO, d), lambda i: (jnp.maximum(i * per - 1, 0), 0)),
                  pl.BlockSpec((tm, d), lambda i: (i, 0)),
                  pl.BlockSpec(w_grp.shape, lambda i: (0, 0, 0)),
                  pl.BlockSpec((1, d), lambda i: (0, 0))],
        out_specs=pl.BlockSpec((tm, d), lambda i: (i, 0)),
        out_shape=jax.ShapeDtypeStruct((s, d), BF16),
        compiler_params=_params(("arbitrary",)),
        name="pool_mix",
    )(v, v, sgate, w_grp, layer_scale)


def _dsa_kernel(q_ref, k_ref, v_ref, sg_ref, qi_ref, kw_ref, kit_ref, o_ref,
                sc_ref, wrep_ref, m_ref, l_ref, acc_ref, lo_ref, hi_ref, clo_ref,
                *, tq, tk, top_k):
    i = pl.program_id(0)
    j = pl.program_id(1)
    last = (i * tq + tq - 1) // tk
    n_lane_tiles = tk // LANES

    def tile_lanes(a):
        return jnp.tile(a, (1, n_lane_tiles))

    @pl.when(j == 0)
    def _index_and_select():
        kw = kw_ref[...]
        for h in range(IDX_HEADS):
            col = IDX_DIM + h
            wrep_ref[h] = jnp.broadcast_to(kw[:, col:col + 1], (tq, LANES))

        row = i * tq + lax.broadcasted_iota(jnp.int32, (tq, tk), 0)

        def score_tile(jt, carry):
            rmax, rmin = carry
            acc = jnp.zeros((tq, tk), F32)
            for p in range(IDX_HEADS // 2):
                lhs = qi_ref[:, p * LANES:(p + 1) * LANES]
                for e in range(2):
                    s = jnp.dot(lhs, kit_ref[e, jt], preferred_element_type=F32)
                    acc = acc + jnp.maximum(s, 0.0) * tile_lanes(wrep_ref[2 * p + e])
            col = jt * tk + lax.broadcasted_iota(jnp.int32, (tq, tk), 1)
            adm = col <= row
            sc_ref[jt] = jnp.where(adm, acc, NEG_MASK)
            hi_part = jnp.where(adm, acc, NEG_MASK)
            lo_part = jnp.where(adm, acc, -NEG_MASK)
            for c in range(n_lane_tiles):
                rmax = jnp.maximum(rmax, hi_part[:, c * LANES:(c + 1) * LANES])
                rmin = jnp.minimum(rmin, lo_part[:, c * LANES:(c + 1) * LANES])
            return rmax, rmin

        rmax, rmin = lax.fori_loop(
            0, last + 1, score_tile,
            (jnp.full((tq, LANES), NEG_MASK, F32), jnp.full((tq, LANES), -NEG_MASK, F32)))
        rmax = jnp.broadcast_to(jnp.max(rmax, axis=1, keepdims=True), (tq, LANES))
        rmin = jnp.broadcast_to(jnp.min(rmin, axis=1, keepdims=True), (tq, LANES))

        n_adm = (i * tq + lax.broadcasted_iota(jnp.int32, (tq, LANES), 0) + 1).astype(F32)
        all_rows = n_adm <= float(top_k)
        lo_ref[...] = jnp.where(all_rows, SELECT_ALL, rmin)
        hi_ref[...] = rmax + (jnp.abs(rmax) * 1e-6 + 1e-30)
        clo_ref[...] = jnp.where(all_rows, float(top_k), n_adm)

        def count_ge(thr):
            def body(jt, cnt):
                t = sc_ref[jt]
                for c in range(n_lane_tiles):
                    cnt = cnt + jnp.where(t[:, c * LANES:(c + 1) * LANES] >= thr, 1.0, 0.0)
                return cnt
            cnt = lax.fori_loop(0, last + 1, body, jnp.zeros((tq, LANES), F32))
            return jnp.broadcast_to(jnp.sum(cnt, axis=1, keepdims=True), (tq, LANES))

        def n_open():
            lo, hi, clo = lo_ref[...], hi_ref[...], clo_ref[...]
            mid = lo + (hi - lo) * 0.5
            is_open = (clo != float(top_k)) & (mid > lo) & (mid < hi)
            return jnp.sum(jnp.where(is_open[:, 0:1], 1.0, 0.0))

        def cond(carry):
            it, open_rows = carry
            return (it < MAX_BISECT) & (open_rows > 0.0)

        def body(carry):
            it, _ = carry
            lo, hi, clo = lo_ref[...], hi_ref[...], clo_ref[...]
            mid = lo + (hi - lo) * 0.5
            is_open = (clo != float(top_k)) & (mid > lo) & (mid < hi)
            cnt = count_ge(mid)
            take = cnt >= float(top_k)
            up = is_open & take
            lo_ref[...] = jnp.where(up, mid, lo)
            clo_ref[...] = jnp.where(up, cnt, clo)
            hi_ref[...] = jnp.where(is_open & jnp.logical_not(take), mid, hi)
            return it + 1, n_open()

        lax.while_loop(cond, body, (jnp.int32(0), n_open()))

        thr = tile_lanes(lo_ref[...])

        def to_bias(jt, _):
            sc_ref[jt] = jnp.where(sc_ref[jt] >= thr, 0.0, NEG_MASK)
            return 0

        lax.fori_loop(0, last + 1, to_bias, 0)

        m_ref[...] = jnp.full(m_ref.shape, NEG_MASK, F32)
        l_ref[...] = jnp.zeros(l_ref.shape, F32)
        acc_ref[...] = jnp.zeros(acc_ref.shape, F32)

    @pl.when(j <= last)
    def _attend():
        bias = sc_ref[j]
        for h in range(N_HEADS):
            hs = slice(h * HEAD_DIM, (h + 1) * HEAD_DIM)
            s = lax.dot_general(q_ref[:, hs], k_ref[:, hs], (((1,), (1,)), ((), ())),
                                preferred_element_type=F32) + bias
            m_prev = m_ref[h]
            m_new = jnp.maximum(m_prev, jnp.max(s, axis=1, keepdims=True))
            alpha = jnp.exp2(m_prev - m_new)
            p = jnp.exp2(s - tile_lanes(m_new))
            l_ref[h] = alpha * l_ref[h] + jnp.sum(p, axis=1, keepdims=True)
            acc_ref[:, hs] = alpha * acc_ref[:, hs] + jnp.dot(
                p.astype(BF16), v_ref[:, hs], preferred_element_type=F32)
            m_ref[h] = m_new

    @pl.when(j == last)
    def _finish():
        for h in range(N_HEADS):
            hs = slice(h * HEAD_DIM, (h + 1) * HEAD_DIM)
            o = acc_ref[:, hs] / l_ref[h]
            o_ref[:, hs] = (o * sg_ref[:, hs].astype(F32)).astype(o_ref.dtype)


def _dsa_attention(qk, vv, sg, qi, kw, kit, *, tq=256, tk=512):
    s = qk.shape[0]
    tq = min(tq, s)
    tk = min(tk, s)
    nkt = s // tk
    top_k = min(TOP_K_MAX, s // 4)
    aw = ATTN_WIDTH

    def kv_tile(i, j):
        return jnp.minimum(j, (i * tq + tq - 1) // tk)

    return pl.pallas_call(
        functools.partial(_dsa_kernel, tq=tq, tk=tk, top_k=top_k),
        grid=(s // tq, nkt),
        in_specs=[
            pl.BlockSpec((tq, aw), lambda i, j: (i, 0)),
            pl.BlockSpec((tk, aw), lambda i, j: (kv_tile(i, j), 1)),
            pl.BlockSpec((tk, aw), lambda i, j: (kv_tile(i, j), 0)),
            pl.BlockSpec((tq, aw), lambda i, j: (i, 0)),
            pl.BlockSpec((tq, IDX_HEADS * IDX_DIM), lambda i, j: (i, 0)),
            pl.BlockSpec((tq, LANES), lambda i, j: (i, 0)),
            pl.BlockSpec(kit.shape, lambda i, j: (0, 0, 0, 0)),
        ],
        out_specs=pl.BlockSpec((tq, aw), lambda i, j: (i, 0)),
        out_shape=jax.ShapeDtypeStruct((s, aw), BF16),
        scratch_shapes=[
            pltpu.VMEM((nkt, tq, tk), F32),
            pltpu.VMEM((IDX_HEADS, tq, LANES), F32),
            pltpu.VMEM((N_HEADS, tq, LANES), F32),
            pltpu.VMEM((N_HEADS, tq, LANES), F32),
            pltpu.VMEM((tq, aw), F32),
            pltpu.VMEM((tq, LANES), F32),
            pltpu.VMEM((tq, LANES), F32),
            pltpu.VMEM((tq, LANES), F32),
        ],
        compiler_params=_params(("arbitrary", "arbitrary")),
        name="dsa_index_select_attend",
    )(qk, qk, vv, sg, qi, kw, kit)


def _pool_layer(x, h, gate_mod, w_in, w_grp, layer_scale, w_out):
    d = D_MODEL
    w_in = w_in.astype(BF16)
    v = _mm(h, w_in[:, :d], epilogue="plain", out_dtype=F32, name="pool_in_value")
    sgate = _mm(h, w_in[:, d:], epilogue="silu", out_dtype=BF16, name="pool_in_gate")
    mixed = _pool_core(v, sgate, w_grp.astype(BF16), layer_scale.reshape(1, d))
    return _mm(mixed, w_out.astype(BF16), epilogue="resid", out_dtype=F32, x=x, gm=gate_mod,
               name="pool_out_residual")


def _dsa_layer(x, h, gate_mod, w_in, w_out, tabs):
    s = x.shape[0]
    aw = ATTN_WIDTH
    nqi = IDX_HEADS * IDX_DIM
    w_qk = w_in[:, :2 * aw].astype(BF16)
    w_v = w_in[:, 2 * aw:3 * aw].astype(BF16)
    w_g = w_in[:, 3 * aw:4 * aw].astype(BF16)
    n_idx = nqi + LANES
    w_idx = jnp.pad(w_in[:, 4 * aw:], ((0, 0), (0, n_idx - (w_in.shape[1] - 4 * aw)))).astype(BF16)

    tn = 512
    qk = _mm(h, w_qk, epilogue="rope", out_dtype=BF16, tn=tn, tab=tabs, tab_base=0,
             tab_split=aw // tn, half=ROT_DIM // 2, name="dsa_in_qk")
    vv = _mm(h, w_v, epilogue="plain", out_dtype=BF16, tn=tn, name="dsa_in_v")
    sg = _mm(h, w_g, epilogue="silu", out_dtype=BF16, tn=tn, name="dsa_in_gate")
    qi, kw = _mm_idx(h, w_idx, tabs, tab_pair=1, half=IDX_ROT // 2)

    tk = min(512, s)
    ki_t = kw[:, :IDX_DIM].astype(BF16).T.reshape(IDX_DIM, s // tk, tk).transpose(1, 0, 2)
    zeros = jnp.zeros_like(ki_t)
    kit = jnp.stack([jnp.concatenate([ki_t, zeros], axis=1),
                     jnp.concatenate([zeros, ki_t], axis=1)])
    og = _dsa_attention(qk, vv, sg, qi, kw, kit, tk=tk)
    return _mm(og, w_out.astype(BF16), epilogue="resid", out_dtype=F32, x=x, gm=gate_mod,
               name="dsa_out_residual")


@jax.jit
def kernel(x, c, positions, norm_g, mod_w, mod_b, pool_w_in, pool_w_grp, pool_scale, pool_w_out,
           attn_w_in, attn_w_out, final_g):
    b, s, d = x.shape
    assert b == 1 and d == D_MODEL
    xs = x[0]
    mod = _modulation(c, mod_w, mod_b)

    attn_scale = HEAD_DIM ** -0.5
    idx_scale = (IDX_HEADS ** -0.5) * (IDX_DIM ** -0.5)
    q_fold = attn_scale * math.log2(math.e)
    consts = jnp.stack([
        _rope_consts(HEAD_DIM, ROT_DIM, q_fold, LANES, 1.0, 0, 0),
        _rope_consts(HEAD_DIM, ROT_DIM, 1.0, LANES, 1.0, 0, 0),
        _rope_consts(IDX_DIM, IDX_ROT, 1.0, LANES, 1.0, 0, 0),
        _rope_consts(IDX_DIM, IDX_ROT, 1.0, IDX_DIM, idx_scale, IDX_DIM, IDX_DIM + IDX_HEADS),
    ])
    tabs = _rope_tables(positions[0], consts)

    zero_row = jnp.zeros((1, d), F32)
    for i in range(DEPTH):
        shift, scale, gate = mod[i, :, :d], mod[i, :, d:2 * d], mod[i, :, 2 * d:]
        h = _norm(xs, norm_g[i].reshape(1, d), scale, shift, modulate=True, out_dtype=BF16)
        jl = i // 2
        if i % 2 == 0:
            xs = _pool_layer(xs, h, gate, pool_w_in[jl], pool_w_grp[jl], pool_scale[jl], pool_w_out[jl])
        else:
            xs = _dsa_layer(xs, h, gate, attn_w_in[jl], attn_w_out[jl], tabs)
    out = _norm(xs, final_g.reshape(1, d), zero_row, zero_row, modulate=False, out_dtype=F32)
    return out[None]
```

```python
import functools
import math

import jax
import jax.numpy as jnp
from jax import lax
from jax.experimental import pallas as pl
from jax.experimental.pallas import tpu as pltpu

F32 = jnp.float32
BF16 = jnp.bfloat16

D_MODEL = 2048
DEPTH = 4
EPS = 1e-6
POOL_WINDOWS = (2, 4, 8, 16)
POOL_GROUP_DIM = D_MODEL // len(POOL_WINDOWS)
POOL_HALO = 16
N_HEADS = 16
HEAD_DIM = 128
ATTN_WIDTH = N_HEADS * HEAD_DIM
ROT_DIM = HEAD_DIM // 4
IDX_HEADS = 16
IDX_DIM = 64
IDX_ROT = IDX_DIM // 4
ROPE_THETA = 500000.0
TOP_K_MAX = 256
LANES = 128
SUBLANES = 8
VMEM_LIMIT = 56 * 1024 * 1024

NEG_MASK = -1e30
SELECT_ALL = -1e29
MAX_BISECT = 64
NT_DIMS = (((1,), (1,)), ((), ()))


def _params(sem):
    return pltpu.CompilerParams(dimension_semantics=sem, vmem_limit_bytes=VMEM_LIMIT)


def _mod_kernel(c_ref, w_ref, b_ref, o_ref, cond_ref, *, kc):
    @pl.when((pl.program_id(0) == 0) & (pl.program_id(1) == 0))
    def _():
        cc = c_ref[...]
        cond_ref[...] = jnp.broadcast_to(cc * jax.nn.sigmoid(cc), cond_ref.shape)

    tn = o_ref.shape[-1]

    def body(i, acc):
        k0 = pl.multiple_of(i * kc, kc)
        wk = w_ref[0, pl.ds(k0, kc), :]
        ck = jnp.tile(cond_ref[pl.ds(k0, kc), :], (1, tn // LANES))
        return acc + jnp.sum((wk * ck).reshape(kc // 8, 8, tn), axis=0)

    acc = lax.fori_loop(0, D_MODEL // kc, body, jnp.zeros((8, tn), F32))
    o_ref[0] = jnp.sum(acc, axis=0, keepdims=True) + b_ref[0]


def _modulation(c, mod_w, mod_b):
    depth, d, n = mod_w.shape
    tn = 512
    out = pl.pallas_call(
        functools.partial(_mod_kernel, kc=64),
        grid=(depth, n // tn),
        in_specs=[
            pl.BlockSpec((d, 1), lambda l, j: (0, 0)),
            pl.BlockSpec((1, d, tn), lambda l, j: (l, 0, j)),
            pl.BlockSpec((1, 1, tn), lambda l, j: (l, 0, j)),
        ],
        out_specs=pl.BlockSpec((1, 1, tn), lambda l, j: (l, 0, j)),
        out_shape=jax.ShapeDtypeStruct((depth, 1, n), F32),
        scratch_shapes=[pltpu.VMEM((d, LANES), F32)],
        compiler_params=_params(("arbitrary", "arbitrary")),
        name="adaln_modulation",
    )(c.reshape(d, 1), mod_w, mod_b.reshape(depth, 1, n))
    return out


def _norm_kernel(x_ref, g_ref, sc_ref, sh_ref, o_ref, *, modulate):
    x = x_ref[...]
    ms = jnp.mean(x * x, axis=-1, keepdims=True)
    y = x * lax.rsqrt(ms + EPS) * g_ref[...]
    if modulate:
        y = y * (1.0 + sc_ref[...]) + sh_ref[...]
    o_ref[...] = y.astype(o_ref.dtype)


def _norm(x, g, scale, shift, *, modulate, out_dtype, tm=256):
    s, d = x.shape
    row = pl.BlockSpec((1, d), lambda i: (0, 0))
    return pl.pallas_call(
        functools.partial(_norm_kernel, modulate=modulate),
        grid=(s // tm,),
        in_specs=[pl.BlockSpec((tm, d), lambda i: (i, 0)), row, row, row],
        out_specs=pl.BlockSpec((tm, d), lambda i: (i, 0)),
        out_shape=jax.ShapeDtypeStruct((s, d), out_dtype),
        compiler_params=_params(("arbitrary",)),
        name="rmsnorm_modulate" if modulate else "rmsnorm_final",
    )(x, g, scale, shift)


def _rope_apply(acc, tab, half):
    tn = acc.shape[1]
    reps = tn // LANES
    c = jnp.tile(tab[:, :LANES], (1, reps))
    s1 = jnp.tile(tab[:, LANES:2 * LANES], (1, reps))
    s2 = jnp.tile(tab[:, 2 * LANES:], (1, reps))
    return (acc * c + pltpu.roll(acc, half, 1) * s1 + pltpu.roll(acc, tn - half, 1) * s2)


def _mm_plain_kernel(a_ref, w_ref, o_ref):
    acc = jnp.dot(a_ref[...], w_ref[...], preferred_element_type=F32)
    o_ref[...] = acc.astype(o_ref.dtype)


def _mm_silu_kernel(a_ref, w_ref, o_ref):
    acc = jnp.dot(a_ref[...], w_ref[...], preferred_element_type=F32)
    o_ref[...] = (acc * jax.nn.sigmoid(acc)).astype(o_ref.dtype)


def _mm_rope_heads_kernel(a_ref, w_ref, tab_ref, o_ref, *, half):
    acc = jnp.dot(a_ref[...], w_ref[...], preferred_element_type=F32)
    out = _rope_apply(acc, tab_ref[0], half).astype(o_ref.dtype)
    for hh in range(o_ref.shape[0]):
        o_ref[hh] = out[:, hh * HEAD_DIM:(hh + 1) * HEAD_DIM]


def _mm_heads_t_kernel(wt_ref, a_ref, o_ref):
    res = lax.dot_general(wt_ref[...], a_ref[...], NT_DIMS, preferred_element_type=F32)
    o_ref[...] = res.reshape(o_ref.shape).astype(o_ref.dtype)


def _mm_resid_kernel(a_ref, w_ref, x_ref, gm_ref, o_ref):
    acc = jnp.dot(a_ref[...], w_ref[...], preferred_element_type=F32)
    o_ref[...] = x_ref[...] + gm_ref[...] * acc


def _mm_idx_kernel(a_ref, w_ref, tab_ref, oq_ref, okw_ref, *, half):
    acc = jnp.dot(a_ref[...], w_ref[...], preferred_element_type=F32)
    nq = oq_ref.shape[1]
    groups = nq // LANES
    tab_q = tab_ref[0]
    tab_kw = tab_ref[1]
    c = jnp.concatenate([tab_q[:, :LANES]] * groups + [tab_kw[:, :LANES]], axis=1)
    s1 = jnp.concatenate([tab_q[:, LANES:2 * LANES]] * groups + [tab_kw[:, LANES:2 * LANES]], axis=1)
    s2 = jnp.concatenate([tab_q[:, 2 * LANES:]] * groups + [tab_kw[:, 2 * LANES:]], axis=1)
    tn = acc.shape[1]
    out = acc * c + pltpu.roll(acc, half, 1) * s1 + pltpu.roll(acc, tn - half, 1) * s2
    oq_ref[...] = out[:, :nq].astype(oq_ref.dtype)
    okw_ref[...] = out[:, nq:]


def _mm(a, w, *, epilogue, out_dtype, tm=1024, tn=512, x=None, gm=None, name="mm"):
    s, k = a.shape
    n = w.shape[1]
    tm = min(tm, s)
    tn = min(tn, n)
    grid = (s // tm, n // tn)
    a_spec = pl.BlockSpec((tm, k), lambda i, j: (i, 0))
    w_spec = pl.BlockSpec((k, tn), lambda i, j: (0, j))
    o_spec = pl.BlockSpec((tm, tn), lambda i, j: (i, j))
    out_shape = jax.ShapeDtypeStruct((s, n), out_dtype)
    if epilogue == "plain":
        kern, ins, specs = _mm_plain_kernel, (a, w), [a_spec, w_spec]
    elif epilogue == "silu":
        kern, ins, specs = _mm_silu_kernel, (a, w), [a_spec, w_spec]
    elif epilogue == "resid":
        kern, ins = _mm_resid_kernel, (a, w, x, gm)
        specs = [a_spec, w_spec, pl.BlockSpec((tm, tn), lambda i, j: (i, j)),
                 pl.BlockSpec((1, tn), lambda i, j: (0, j))]
    else:
        raise ValueError(epilogue)
    return pl.pallas_call(
        kern, grid=grid, in_specs=specs, out_specs=o_spec, out_shape=out_shape,
        compiler_params=_params(("arbitrary", "arbitrary")), name=name,
    )(*ins)


def _mm_rope_heads(a, w, tab, *, tab_base, tab_split, half, tm=1024, tn=512, name="mm_rope"):
    s, k = a.shape
    n = w.shape[1]
    tm = min(tm, s)
    hpt = tn // HEAD_DIM
    return pl.pallas_call(
        functools.partial(_mm_rope_heads_kernel, half=half),
        grid=(s // tm, n // tn),
        in_specs=[pl.BlockSpec((tm, k), lambda i, j: (i, 0)),
                  pl.BlockSpec((k, tn), lambda i, j: (0, j)),
                  pl.BlockSpec((1, tm, 3 * LANES), lambda i, j: (tab_base + j // tab_split, i, 0))],
        out_specs=pl.BlockSpec((hpt, tm, HEAD_DIM), lambda i, j: (j, i, 0)),
        out_shape=jax.ShapeDtypeStruct((n // HEAD_DIM, s, HEAD_DIM), BF16),
        compiler_params=_params(("arbitrary", "arbitrary")), name=name,
    )(a, w, tab)


def _mm_heads_t(wt, a, *, tm=512, tn=512, name="mm_heads_t"):
    s, k = a.shape
    n = wt.shape[0]
    tm = min(tm, s)
    hpt = tn // HEAD_DIM
    return pl.pallas_call(
        _mm_heads_t_kernel,
        grid=(s // tm, n // tn),
        in_specs=[pl.BlockSpec((tn, k), lambda i, j: (j, 0)),
                  pl.BlockSpec((tm, k), lambda i, j: (i, 0))],
        out_specs=pl.BlockSpec((hpt, HEAD_DIM, tm), lambda i, j: (j, 0, i)),
        out_shape=jax.ShapeDtypeStruct((n // HEAD_DIM, HEAD_DIM, s), BF16),
        compiler_params=_params(("arbitrary", "arbitrary")), name=name,
    )(wt, a)


def _mm_idx(a, w, tab, *, tab_pair, half, tm=512):
    s, k = a.shape
    n = w.shape[1]
    nq = IDX_HEADS * IDX_DIM
    tm = min(tm, s)
    return pl.pallas_call(
        functools.partial(_mm_idx_kernel, half=half),
        grid=(s // tm,),
        in_specs=[pl.BlockSpec((tm, k), lambda i: (i, 0)),
                  pl.BlockSpec((k, n), lambda i: (0, 0)),
                  pl.BlockSpec((2, tm, 3 * LANES), lambda i: (tab_pair, i, 0))],
        out_specs=[pl.BlockSpec((tm, nq), lambda i: (i, 0)),
                   pl.BlockSpec((tm, LANES), lambda i: (i, 0))],
        out_shape=[jax.ShapeDtypeStruct((s, nq), BF16),
                   jax.ShapeDtypeStruct((s, LANES), F32)],
        compiler_params=_params(("arbitrary",)),
        name="indexer_projection",
    )(a, w, tab)


def _rope_table_kernel(pos_ref, cst_ref, o_ref):
    pos = pos_ref[...].astype(F32)
    for v in range(o_ref.shape[0]):
        cst = cst_ref[v]
        ang = pos * cst[0:1, :]
        cos = jnp.cos(ang)
        sin = jnp.sin(ang)
        first, second, base, scale = cst[1:2, :], cst[2:3, :], cst[3:4, :], cst[4:5, :]
        rot = first + second
        o_ref[v, :, 0:LANES] = (rot * cos + (1.0 - rot) * base) * scale
        o_ref[v, :, LANES:2 * LANES] = second * sin * scale
        o_ref[v, :, 2 * LANES:3 * LANES] = -(first * sin) * scale


def _rope_tables(positions, consts, tm=512):
    s = positions.shape[0]
    nv = consts.shape[0]
    tm = min(tm, s)
    return pl.pallas_call(
        _rope_table_kernel,
        grid=(s // tm,),
        in_specs=[pl.BlockSpec((tm, 1), lambda i: (i, 0)),
                  pl.BlockSpec((nv, 8, LANES), lambda i: (0, 0, 0))],
        out_specs=pl.BlockSpec((nv, tm, 3 * LANES), lambda i: (0, i, 0)),
        out_shape=jax.ShapeDtypeStruct((nv, s, 3 * LANES), F32),
        compiler_params=_params(("arbitrary",)),
        name="rope_tables",
    )(positions.reshape(s, 1), consts)


def _rope_consts(head_dim, rot_dim, scale, n_rot_lanes, base_val, base_lo, base_hi):
    half = rot_dim // 2
    inv_freq = ROPE_THETA ** (-jnp.arange(half, dtype=F32) * 2.0 / rot_dim)
    lane = jnp.arange(LANES)
    inner = lane % head_dim
    active = lane < n_rot_lanes
    first = (inner < half) & active
    second = (inner >= half) & (inner < rot_dim) & active
    invf = jnp.where(first | second, inv_freq[inner % half], 0.0)
    base = jnp.where((lane >= base_lo) & (lane < base_hi), base_val, 1.0)
    rows = [invf, first.astype(F32), second.astype(F32), base.astype(F32),
            jnp.full((LANES,), scale, F32)]
    rows += [jnp.zeros((LANES,), F32)] * (8 - len(rows))
    return jnp.stack(rows).astype(F32)


def _pool_kernel(v_ref, halo_ref, sg_ref, wg_ref, ls_ref, o_ref, *, tm):
    i = pl.program_id(0)
    keep = (i > 0).astype(F32)
    row = i * tm + lax.broadcasted_iota(jnp.int32, (tm, 1), 0)
    gd = POOL_GROUP_DIM
    for g, w in enumerate(POOL_WINDOWS):
        sl = slice(g * gd, (g + 1) * gd)
        vg = v_ref[:, sl]
        cat = jnp.concatenate([halo_ref[:, sl] * keep, vg], axis=0)
        step = 1
        while step < w:
            cat = cat + pltpu.roll(cat, step, 0)
            step *= 2
        win = cat[POOL_HALO:, :]
        cnt = jnp.minimum(row + 1, w).astype(F32)
        pooled = (win / cnt - vg).astype(BF16)
        mixed = jnp.dot(pooled, wg_ref[g], preferred_element_type=F32)
        mixed = mixed * ls_ref[:, sl] * sg_ref[:, sl].astype(F32)
        o_ref[:, sl] = mixed.astype(o_ref.dtype)


def _pool_core(v, sgate, w_grp, layer_scale, tm=256):
    s, d = v.shape
    tm = min(tm, s)
    per = tm // POOL_HALO
    return pl.pallas_call(
        functools.partial(_pool_kernel, tm=tm),
        grid=(s // tm,),
        in_specs=[pl.BlockSpec((tm, d), lambda i: (i, 0)),
                  pl.BlockSpec((POOL_HALO, d), lambda i: (jnp.maximum(i * per - 1, 0), 0)),
                  pl.BlockSpec((tm, d), lambda i: (i, 0)),
                  pl.BlockSpec(w_grp.shape, lambda i: (0, 0, 0)),
                  pl.BlockSpec((1, d), lambda i: (0, 0))],
        out_specs=pl.BlockSpec((tm, d), lambda i: (i, 0)),
        out_shape=jax.ShapeDtypeStruct((s, d), BF16),
        compiler_params=_params(("arbitrary",)),
        name="pool_mix",
    )(v, v, sgate, w_grp, layer_scale)


def _sublane_all(op, x):
    for shift in (4, 2, 1):
        x = op(x, pltpu.roll(x, shift, 0))
    return x


def _fold_groups(reduce_fn, x, chains=4):
    g = x.shape[0]
    chains = min(chains, g)
    return reduce_fn(reduce_fn(x.reshape(g // chains, chains, *x.shape[1:]), axis=0), axis=0)


def _dsa_kernel(q_ref, k_ref, vt_ref, sg_ref, qi_ref, wt_ref, kip_ref, o_ref,
                sc_ref, s_ref, p_ref, a_ref, m_ref, l_ref, acc_ref, lo_ref, hi_ref, clo_ref,
                *, tq, tk, ksub, top_k):
    i = pl.program_id(0)
    j = pl.program_id(1)
    last = (i * tq + tq - 1) // tk
    sub = SUBLANES

    @pl.when(j == 0)
    def _index_and_select():
        qpos = i * tq + lax.broadcasted_iota(jnp.int32, (ksub, tq), 1)

        def score_sub(u, carry):
            rmax, rmin = carry
            k0 = pl.multiple_of(u * ksub, ksub)
            acc = jnp.zeros((ksub, tq), F32)
            lhs = kip_ref[u]
            for p in range(IDX_HEADS // 2):
                rhs = qi_ref[:, p * LANES:(p + 1) * LANES]
                s = lax.dot_general(lhs, rhs, NT_DIMS, preferred_element_type=F32)
                acc = acc + jnp.maximum(s[:ksub], 0.0) * wt_ref[2 * p:2 * p + 1, :]
                acc = acc + jnp.maximum(s[ksub:], 0.0) * wt_ref[2 * p + 1:2 * p + 2, :]
            kpos = u * ksub + lax.broadcasted_iota(jnp.int32, (ksub, tq), 0)
            adm = kpos <= qpos
            sc_ref[pl.ds(k0, ksub), :] = jnp.where(adm, acc, NEG_MASK)
            hi_part = jnp.where(adm, acc, NEG_MASK).reshape(ksub // sub, sub, tq)
            lo_part = jnp.where(adm, acc, -NEG_MASK).reshape(ksub // sub, sub, tq)
            return (jnp.maximum(rmax, _fold_groups(jnp.max, hi_part)),
                    jnp.minimum(rmin, _fold_groups(jnp.min, lo_part)))

        def score_tile(jt, carry):
            for r in range(tk // ksub):
                carry = score_sub(jt * (tk // ksub) + r, carry)
            return carry

        rmax, rmin = lax.fori_loop(
            0, last + 1, score_tile,
            (jnp.full((sub, tq), NEG_MASK, F32), jnp.full((sub, tq), -NEG_MASK, F32)))
        rmax = _sublane_all(jnp.maximum, rmax)
        rmin = _sublane_all(jnp.minimum, rmin)

        n_adm = (i * tq + lax.broadcasted_iota(jnp.int32, (sub, tq), 1) + 1).astype(F32)
        all_keys = n_adm <= float(top_k)
        lo_ref[...] = jnp.where(all_keys, SELECT_ALL, rmin)
        hi_ref[...] = rmax + (jnp.abs(rmax) * 1e-6 + 1e-30)
        clo_ref[...] = jnp.where(all_keys, float(top_k), n_adm)

        def count_ge(thr):
            def body(jt, cnt):
                k0 = pl.multiple_of(jt * tk, tk)
                t = sc_ref[pl.ds(k0, tk), :].reshape(tk // sub, sub, tq)
                return cnt + _fold_groups(jnp.sum, jnp.where(t >= thr[None], 1.0, 0.0))
            cnt = lax.fori_loop(0, last + 1, body, jnp.zeros((sub, tq), F32))
            return _sublane_all(jnp.add, cnt)

        def n_open():
            lo, hi, clo = lo_ref[...], hi_ref[...], clo_ref[...]
            mid = lo + (hi - lo) * 0.5
            is_open = (clo != float(top_k)) & (mid > lo) & (mid < hi)
            return jnp.sum(jnp.where(is_open[0:1, :], 1.0, 0.0))

        def cond(carry):
            it, open_rows = carry
            return (it < MAX_BISECT) & (open_rows > 0.0)

        def body(carry):
            it, _ = carry
            lo, hi, clo = lo_ref[...], hi_ref[...], clo_ref[...]
            mid = lo + (hi - lo) * 0.5
            is_open = (clo != float(top_k)) & (mid > lo) & (mid < hi)
            cnt = count_ge(mid)
            take = cnt >= float(top_k)
            up = is_open & take
            lo_ref[...] = jnp.where(up, mid, lo)
            clo_ref[...] = jnp.where(up, cnt, clo)
            hi_ref[...] = jnp.where(is_open & jnp.logical_not(take), mid, hi)
            return it + 1, n_open()

        lax.while_loop(cond, body, (jnp.int32(0), n_open()))

        thr = lo_ref[...]

        def to_bias(jt, _):
            k0 = pl.multiple_of(jt * tk, tk)
            t = sc_ref[pl.ds(k0, tk), :].reshape(tk // sub, sub, tq)
            sc_ref[pl.ds(k0, tk), :] = jnp.where(t >= thr[None], 0.0, NEG_MASK).reshape(tk, tq)
            return 0

        lax.fori_loop(0, last + 1, to_bias, 0)

        m_ref[...] = jnp.full(m_ref.shape, NEG_MASK, F32)
        l_ref[...] = jnp.zeros(l_ref.shape, F32)
        acc_ref[...] = jnp.zeros(acc_ref.shape, F32)

    @pl.when(j <= last)
    def _attend():
        kbase = pl.multiple_of(j * tk, tk)

        def logits(h, slot):
            s_ref[slot] = lax.dot_general(k_ref[h], q_ref[h], NT_DIMS, preferred_element_type=F32)

        def softmax(h, slot):
            x = (s_ref[slot] + sc_ref[pl.ds(kbase, tk), :]).reshape(tk // sub, sub, tq)
            m_prev = m_ref[h]
            m_new = jnp.maximum(m_prev, _sublane_all(jnp.maximum, _fold_groups(jnp.max, x)))
            alpha = jnp.exp2(m_prev - m_new)
            p = jnp.exp2(x - m_new[None])
            l_ref[h] = alpha * l_ref[h] + _fold_groups(jnp.sum, p)
            p_ref[slot] = p.reshape(tk, tq).astype(BF16)
            a_ref[slot] = alpha
            m_ref[h] = m_new

        def weighted_values(h, slot):
            acc = acc_ref[h].reshape(HEAD_DIM // sub, sub, tq) * a_ref[slot][None]
            acc_ref[h] = acc.reshape(HEAD_DIM, tq) + jnp.dot(
                vt_ref[h], p_ref[slot], preferred_element_type=F32)

        logits(0, 0)
        logits(1, 1)
        softmax(0, 0)

        for h in range(N_HEADS - 2):
            logits(h + 2, h % 2)
            softmax(h + 1, (h + 1) % 2)
            weighted_values(h, h % 2)
        softmax(N_HEADS - 1, 1)
        weighted_values(N_HEADS - 2, 0)
        weighted_values(N_HEADS - 1, 1)

    @pl.when(j == last)
    def _finish():
        for h in range(N_HEADS):
            hs = slice(h * HEAD_DIM, (h + 1) * HEAD_DIM)
            l = _sublane_all(jnp.add, l_ref[h])
            o_t = (acc_ref[h].reshape(HEAD_DIM // sub, sub, tq) / l[None]).reshape(HEAD_DIM, tq)
            o_ref[:, hs] = (o_t.T * sg_ref[:, hs].astype(F32)).astype(o_ref.dtype)


def _dsa_attention(qk, vt, sg, qi, wt, kip, *, tq=256, tk=512, ksub=128):
    s = sg.shape[0]
    tq = min(tq, s)
    tk = min(tk, s)
    top_k = min(TOP_K_MAX, s // 4)
    aw = ATTN_WIDTH

    def kv_tile(i, j):
        return jnp.minimum(j, (i * tq + tq - 1) // tk)

    return pl.pallas_call(
        functools.partial(_dsa_kernel, tq=tq, tk=tk, ksub=ksub, top_k=top_k),
        grid=(s // tq, s // tk),
        in_specs=[
            pl.BlockSpec((N_HEADS, tq, HEAD_DIM), lambda i, j: (0, i, 0)),
            pl.BlockSpec((N_HEADS, tk, HEAD_DIM), lambda i, j: (1, kv_tile(i, j), 0)),
            pl.BlockSpec((N_HEADS, HEAD_DIM, tk), lambda i, j: (0, 0, kv_tile(i, j))),
            pl.BlockSpec((tq, aw), lambda i, j: (i, 0)),
            pl.BlockSpec((tq, IDX_HEADS * IDX_DIM), lambda i, j: (i, 0)),
            pl.BlockSpec((IDX_HEADS, tq), lambda i, j: (0, i)),
            pl.BlockSpec(kip.shape, lambda i, j: (0, 0, 0)),
        ],
        out_specs=pl.BlockSpec((tq, aw), lambda i, j: (i, 0)),
        out_shape=jax.ShapeDtypeStruct((s, aw), BF16),
        scratch_shapes=[
            pltpu.VMEM((s, tq), F32),
            pltpu.VMEM((2, tk, tq), F32),
            pltpu.VMEM((2, tk, tq), BF16),
            pltpu.VMEM((2, SUBLANES, tq), F32),
            pltpu.VMEM((N_HEADS, SUBLANES, tq), F32),
            pltpu.VMEM((N_HEADS, SUBLANES, tq), F32),
            pltpu.VMEM((N_HEADS, HEAD_DIM, tq), F32),
            pltpu.VMEM((SUBLANES, tq), F32),
            pltpu.VMEM((SUBLANES, tq), F32),
            pltpu.VMEM((SUBLANES, tq), F32),
        ],
        compiler_params=_params(("arbitrary", "arbitrary")),
        name="dsa_index_select_attend",
    )(qk, qk, vt, sg, qi, wt, kip)


def _pool_layer(x, h, gate_mod, w_in, w_grp, layer_scale, w_out):
    d = D_MODEL
    w_in = w_in.astype(BF16)
    v = _mm(h, w_in[:, :d], epilogue="plain", out_dtype=F32, name="pool_in_value")
    sgate = _mm(h, w_in[:, d:], epilogue="silu", out_dtype=BF16, name="pool_in_gate")
    mixed = _pool_core(v, sgate, w_grp.astype(BF16), layer_scale.reshape(1, d))
    return _mm(mixed, w_out.astype(BF16), epilogue="resid", out_dtype=F32, x=x, gm=gate_mod,
               name="pool_out_residual")


def _dsa_layer(x, h, gate_mod, w_in, w_out, tabs):
    aw = ATTN_WIDTH
    nqi = IDX_HEADS * IDX_DIM
    w_qk = w_in[:, :2 * aw].astype(BF16)
    w_vt = w_in[:, 2 * aw:3 * aw].T.astype(BF16)
    w_g = w_in[:, 3 * aw:4 * aw].astype(BF16)
    n_idx = nqi + LANES
    w_idx = jnp.pad(w_in[:, 4 * aw:], ((0, 0), (0, n_idx - (w_in.shape[1] - 4 * aw)))).astype(BF16)

    tn = 512
    qk = _mm_rope_heads(h, w_qk, tabs, tab_base=0, tab_split=aw // tn, half=ROT_DIM // 2, tn=tn,
                        name="dsa_in_qk")
    vt = _mm_heads_t(w_vt, h, name="dsa_in_vt")
    sg = _mm(h, w_g, epilogue="silu", out_dtype=BF16, tn=tn, name="dsa_in_gate")
    qi, kw = _mm_idx(h, w_idx, tabs, tab_pair=1, half=IDX_ROT // 2)

    ki = kw[:, :IDX_DIM].astype(BF16)
    zeros = jnp.zeros_like(ki)
    ksub = 128
    nsub = ki.shape[0] // ksub
    kip = jnp.concatenate([jnp.concatenate([ki, zeros], axis=1).reshape(nsub, ksub, LANES),
                           jnp.concatenate([zeros, ki], axis=1).reshape(nsub, ksub, LANES)],
                          axis=1)
    wt = kw[:, IDX_DIM:IDX_DIM + IDX_HEADS].T
    og = _dsa_attention(qk, vt, sg, qi, wt, kip, ksub=ksub)
    return _mm(og, w_out.astype(BF16), epilogue="resid", out_dtype=F32, x=x, gm=gate_mod,
               name="dsa_out_residual")


@jax.jit
def kernel(x, c, positions, norm_g, mod_w, mod_b, pool_w_in, pool_w_grp, pool_scale, pool_w_out,
           attn_w_in, attn_w_out, final_g):
    b, s, d = x.shape
    assert b == 1 and d == D_MODEL
    xs = x[0]
    mod = _modulation(c, mod_w, mod_b)

    attn_scale = HEAD_DIM ** -0.5
    idx_scale = (IDX_HEADS ** -0.5) * (IDX_DIM ** -0.5)
    q_fold = attn_scale * math.log2(math.e)
    consts = jnp.stack([
        _rope_consts(HEAD_DIM, ROT_DIM, q_fold, LANES, 1.0, 0, 0),
        _rope_consts(HEAD_DIM, ROT_DIM, 1.0, LANES, 1.0, 0, 0),
        _rope_consts(IDX_DIM, IDX_ROT, 1.0, LANES, 1.0, 0, 0),
        _rope_consts(IDX_DIM, IDX_ROT, 1.0, IDX_DIM, idx_scale, IDX_DIM, IDX_DIM + IDX_HEADS),
    ])
    tabs = _rope_tables(positions[0], consts)

    zero_row = jnp.zeros((1, d), F32)
    for i in range(DEPTH):
        shift, scale, gate = mod[i, :, :d], mod[i, :, d:2 * d], mod[i, :, 2 * d:]
        h = _norm(xs, norm_g[i].reshape(1, d), scale, shift, modulate=True, out_dtype=BF16)
        jl = i // 2
        if i % 2 == 0:
            xs = _pool_layer(xs, h, gate, pool_w_in[jl], pool_w_grp[jl], pool_scale[jl], pool_w_out[jl])
        else:
            xs = _dsa_layer(xs, h, gate, attn_w_in[jl], attn_w_out[jl], tabs)
    out = _norm(xs, final_g.reshape(1, d), zero_row, zero_row, modulate=False, out_dtype=F32)
    return out[None]
```

```python
import functools
import math

import jax
import jax.numpy as jnp
from jax import lax
from jax.experimental import pallas as pl
from jax.experimental.pallas import tpu as pltpu

F32 = jnp.float32
BF16 = jnp.bfloat16

D_MODEL = 2048
DEPTH = 4
EPS = 1e-6
POOL_WINDOWS = (2, 4, 8, 16)
POOL_GROUP_DIM = D_MODEL // len(POOL_WINDOWS)
POOL_HALO = 16
N_HEADS = 16
HEAD_DIM = 128
ATTN_WIDTH = N_HEADS * HEAD_DIM
ROT_DIM = HEAD_DIM // 4
IDX_HEADS = 16
IDX_DIM = 64
IDX_ROT = IDX_DIM // 4
ROPE_THETA = 500000.0
TOP_K_MAX = 256
LANES = 128
SUBLANES = 8
VMEM_LIMIT = 56 * 1024 * 1024

NEG_MASK = -1e30
SELECT_ALL = -1e29
MAX_BISECT = 64
NT_DIMS = (((1,), (1,)), ((), ()))


def _params(sem):
    return pltpu.CompilerParams(dimension_semantics=sem, vmem_limit_bytes=VMEM_LIMIT)


def _mod_kernel(c_ref, w_ref, b_ref, o_ref, cond_ref, *, kc):
    @pl.when((pl.program_id(0) == 0) & (pl.program_id(1) == 0))
    def _():
        cc = c_ref[...]
        cond_ref[...] = jnp.broadcast_to(cc * jax.nn.sigmoid(cc), cond_ref.shape)

    tn = o_ref.shape[-1]

    def body(i, acc):
        k0 = pl.multiple_of(i * kc, kc)
        wk = w_ref[0, pl.ds(k0, kc), :]
        ck = jnp.tile(cond_ref[pl.ds(k0, kc), :], (1, tn // LANES))
        return acc + jnp.sum((wk * ck).reshape(kc // 8, 8, tn), axis=0)

    acc = lax.fori_loop(0, D_MODEL // kc, body, jnp.zeros((8, tn), F32))
    o_ref[0] = jnp.sum(acc, axis=0, keepdims=True) + b_ref[0]


def _modulation(c, mod_w, mod_b):
    depth, d, n = mod_w.shape
    tn = 512
    out = pl.pallas_call(
        functools.partial(_mod_kernel, kc=64),
        grid=(depth, n // tn),
        in_specs=[
            pl.BlockSpec((d, 1), lambda l, j: (0, 0)),
            pl.BlockSpec((1, d, tn), lambda l, j: (l, 0, j)),
            pl.BlockSpec((1, 1, tn), lambda l, j: (l, 0, j)),
        ],
        out_specs=pl.BlockSpec((1, 1, tn), lambda l, j: (l, 0, j)),
        out_shape=jax.ShapeDtypeStruct((depth, 1, n), F32),
        scratch_shapes=[pltpu.VMEM((d, LANES), F32)],
        compiler_params=_params(("arbitrary", "arbitrary")),
        name="adaln_modulation",
    )(c.reshape(d, 1), mod_w, mod_b.reshape(depth, 1, n))
    return out


def _norm_kernel(x_ref, g_ref, sc_ref, sh_ref, o_ref, *, modulate):
    x = x_ref[...]
    ms = jnp.mean(x * x, axis=-1, keepdims=True)
    y = x * lax.rsqrt(ms + EPS) * g_ref[...]
    if modulate:
        y = y * (1.0 + sc_ref[...]) + sh_ref[...]
    o_ref[...] = y.astype(o_ref.dtype)


def _norm(x, g, scale, shift, *, modulate, out_dtype, tm=256):
    s, d = x.shape
    row = pl.BlockSpec((1, d), lambda i: (0, 0))
    return pl.pallas_call(
        functools.partial(_norm_kernel, modulate=modulate),
        grid=(s // tm,),
        in_specs=[pl.BlockSpec((tm, d), lambda i: (i, 0)), row, row, row],
        out_specs=pl.BlockSpec((tm, d), lambda i: (i, 0)),
        out_shape=jax.ShapeDtypeStruct((s, d), out_dtype),
        compiler_params=_params(("arbitrary",)),
        name="rmsnorm_modulate" if modulate else "rmsnorm_final",
    )(x, g, scale, shift)


def _rope_apply(acc, tab, half):
    tn = acc.shape[1]
    reps = tn // LANES
    c = jnp.tile(tab[:, :LANES], (1, reps))
    s1 = jnp.tile(tab[:, LANES:2 * LANES], (1, reps))
    s2 = jnp.tile(tab[:, 2 * LANES:], (1, reps))
    return (acc * c + pltpu.roll(acc, half, 1) * s1 + pltpu.roll(acc, tn - half, 1) * s2)


def _mm_plain_kernel(a_ref, w_ref, o_ref):
    acc = jnp.dot(a_ref[...], w_ref[...], preferred_element_type=F32)
    o_ref[...] = acc.astype(o_ref.dtype)


def _mm_silu_kernel(a_ref, w_ref, o_ref):
    acc = jnp.dot(a_ref[...], w_ref[...], preferred_element_type=F32)
    o_ref[...] = (acc * jax.nn.sigmoid(acc)).astype(o_ref.dtype)


def _mm_rope_heads_kernel(a_ref, w_ref, tab_ref, o_ref, *, half):
    a = a_ref[...]
    tab = tab_ref[0]
    for c in range(o_ref.shape[0] // 2):
        acc = jnp.dot(a, w_ref[:, 2 * c * HEAD_DIM:2 * (c + 1) * HEAD_DIM], preferred_element_type=F32)
        out = _rope_apply(acc, tab, half).astype(o_ref.dtype)
        o_ref[2 * c] = out[:, :HEAD_DIM]
        o_ref[2 * c + 1] = out[:, HEAD_DIM:]


def _mm_heads_t_kernel(wt_ref, a_ref, o_ref):
    res = lax.dot_general(wt_ref[...], a_ref[...], NT_DIMS, preferred_element_type=F32)
    o_ref[...] = res.reshape(o_ref.shape).astype(o_ref.dtype)


def _mm_resid_kernel(a_ref, w_ref, x_ref, gm_ref, o_ref):
    acc = jnp.dot(a_ref[...], w_ref[...], preferred_element_type=F32)
    o_ref[...] = x_ref[...] + gm_ref[...] * acc


def _mm_idx_kernel(a_ref, w_ref, tab_ref, oq_ref, okw_ref, *, half):
    acc = jnp.dot(a_ref[...], w_ref[...], preferred_element_type=F32)
    nq = oq_ref.shape[1]
    groups = nq // LANES
    tab_q = tab_ref[0]
    tab_kw = tab_ref[1]
    c = jnp.concatenate([tab_q[:, :LANES]] * groups + [tab_kw[:, :LANES]], axis=1)
    s1 = jnp.concatenate([tab_q[:, LANES:2 * LANES]] * groups + [tab_kw[:, LANES:2 * LANES]], axis=1)
    s2 = jnp.concatenate([tab_q[:, 2 * LANES:]] * groups + [tab_kw[:, 2 * LANES:]], axis=1)
    tn = acc.shape[1]
    out = acc * c + pltpu.roll(acc, half, 1) * s1 + pltpu.roll(acc, tn - half, 1) * s2
    oq_ref[...] = out[:, :nq].astype(oq_ref.dtype)
    okw_ref[...] = out[:, nq:]


def _mm(a, w, *, epilogue, out_dtype, tm=1024, tn=1024, x=None, gm=None, name="mm"):
    s, k = a.shape
    n = w.shape[1]
    tm = min(tm, s)
    tn = min(tn, n)
    grid = (s // tm, n // tn)
    a_spec = pl.BlockSpec((tm, k), lambda i, j: (i, 0))
    w_spec = pl.BlockSpec((k, tn), lambda i, j: (0, j))
    o_spec = pl.BlockSpec((tm, tn), lambda i, j: (i, j))
    out_shape = jax.ShapeDtypeStruct((s, n), out_dtype)
    if epilogue == "plain":
        kern, ins, specs = _mm_plain_kernel, (a, w), [a_spec, w_spec]
    elif epilogue == "silu":
        kern, ins, specs = _mm_silu_kernel, (a, w), [a_spec, w_spec]
    elif epilogue == "resid":
        kern, ins = _mm_resid_kernel, (a, w, x, gm)
        specs = [a_spec, w_spec, pl.BlockSpec((tm, tn), lambda i, j: (i, j)),
                 pl.BlockSpec((1, tn), lambda i, j: (0, j))]
    else:
        raise ValueError(epilogue)
    return pl.pallas_call(
        kern, grid=grid, in_specs=specs, out_specs=o_spec, out_shape=out_shape,
        compiler_params=_params(("arbitrary", "arbitrary")), name=name,
    )(*ins)


def _mm_rope_heads(a, w, tab, *, tab_base, tab_split, half, tm=1024, tn=512, name="mm_rope"):
    s, k = a.shape
    n = w.shape[1]
    tm = min(tm, s)
    hpt = tn // HEAD_DIM
    return pl.pallas_call(
        functools.partial(_mm_rope_heads_kernel, half=half),
        grid=(s // tm, n // tn),
        in_specs=[pl.BlockSpec((tm, k), lambda i, j: (i, 0)),
                  pl.BlockSpec((k, tn), lambda i, j: (0, j)),
                  pl.BlockSpec((1, tm, 3 * LANES), lambda i, j: (tab_base + j // tab_split, i, 0))],
        out_specs=pl.BlockSpec((hpt, tm, HEAD_DIM), lambda i, j: (j, i, 0)),
        out_shape=jax.ShapeDtypeStruct((n // HEAD_DIM, s, HEAD_DIM), BF16),
        compiler_params=_params(("arbitrary", "arbitrary")), name=name,
    )(a, w, tab)


def _mm_heads_t(wt, a, *, tm=512, tn=512, name="mm_heads_t"):
    s, k = a.shape
    n = wt.shape[0]
    tm = min(tm, s)
    hpt = tn // HEAD_DIM
    return pl.pallas_call(
        _mm_heads_t_kernel,
        grid=(s // tm, n // tn),
        in_specs=[pl.BlockSpec((tn, k), lambda i, j: (j, 0)),
                  pl.BlockSpec((tm, k), lambda i, j: (i, 0))],
        out_specs=pl.BlockSpec((hpt, HEAD_DIM, tm), lambda i, j: (j, 0, i)),
        out_shape=jax.ShapeDtypeStruct((n // HEAD_DIM, HEAD_DIM, s), BF16),
        compiler_params=_params(("arbitrary", "arbitrary")), name=name,
    )(wt, a)


def _mm_idx(a, w, tab, *, tab_pair, half, tm=512):
    s, k = a.shape
    n = w.shape[1]
    nq = IDX_HEADS * IDX_DIM
    tm = min(tm, s)
    return pl.pallas_call(
        functools.partial(_mm_idx_kernel, half=half),
        grid=(s // tm,),
        in_specs=[pl.BlockSpec((tm, k), lambda i: (i, 0)),
                  pl.BlockSpec((k, n), lambda i: (0, 0)),
                  pl.BlockSpec((2, tm, 3 * LANES), lambda i: (tab_pair, i, 0))],
        out_specs=[pl.BlockSpec((tm, nq), lambda i: (i, 0)),
                   pl.BlockSpec((tm, LANES), lambda i: (i, 0))],
        out_shape=[jax.ShapeDtypeStruct((s, nq), BF16),
                   jax.ShapeDtypeStruct((s, LANES), F32)],
        compiler_params=_params(("arbitrary",)),
        name="indexer_projection",
    )(a, w, tab)


def _rope_table_kernel(pos_ref, cst_ref, o_ref):
    pos = pos_ref[...].astype(F32)
    for v in range(o_ref.shape[0]):
        cst = cst_ref[v]
        if v % 2 == 0:
            ang = pos * cst[0:1, :]
            cos = jnp.cos(ang)
            sin = jnp.sin(ang)
        first, second, base, scale = cst[1:2, :], cst[2:3, :], cst[3:4, :], cst[4:5, :]
        rot = first + second
        o_ref[v, :, 0:LANES] = (rot * cos + (1.0 - rot) * base) * scale
        o_ref[v, :, LANES:2 * LANES] = second * sin * scale
        o_ref[v, :, 2 * LANES:3 * LANES] = -(first * sin) * scale


def _rope_tables(positions, consts, tm=512):
    s = positions.shape[0]
    nv = consts.shape[0]
    tm = min(tm, s)
    return pl.pallas_call(
        _rope_table_kernel,
        grid=(s // tm,),
        in_specs=[pl.BlockSpec((tm, 1), lambda i: (i, 0)),
                  pl.BlockSpec((nv, 8, LANES), lambda i: (0, 0, 0))],
        out_specs=pl.BlockSpec((nv, tm, 3 * LANES), lambda i: (0, i, 0)),
        out_shape=jax.ShapeDtypeStruct((nv, s, 3 * LANES), F32),
        compiler_params=_params(("arbitrary",)),
        name="rope_tables",
    )(positions.reshape(s, 1), consts)


def _rope_consts(head_dim, rot_dim, scale, n_rot_lanes, base_val, base_lo, base_hi):
    half = rot_dim // 2
    inv_freq = ROPE_THETA ** (-jnp.arange(half, dtype=F32) * 2.0 / rot_dim)
    lane = jnp.arange(LANES)
    inner = lane % head_dim
    active = lane < n_rot_lanes
    first = (inner < half) & active
    second = (inner >= half) & (inner < rot_dim) & active
    invf = jnp.where(first | second, inv_freq[inner % half], 0.0)
    base = jnp.where((lane >= base_lo) & (lane < base_hi), base_val, 1.0)
    rows = [invf, first.astype(F32), second.astype(F32), base.astype(F32),
            jnp.full((LANES,), scale, F32)]
    rows += [jnp.zeros((LANES,), F32)] * (8 - len(rows))
    return jnp.stack(rows).astype(F32)


def _pool_kernel(v_ref, halo_ref, sg_ref, wg_ref, ls_ref, o_ref, *, tm):
    i = pl.program_id(0)
    keep = (i > 0).astype(F32)
    row = i * tm + lax.broadcasted_iota(jnp.int32, (tm, 1), 0)
    gd = POOL_GROUP_DIM
    for g, w in enumerate(POOL_WINDOWS):
        sl = slice(g * gd, (g + 1) * gd)
        vg = v_ref[:, sl]
        cat = jnp.concatenate([halo_ref[:, sl] * keep, vg], axis=0)
        step = 1
        while step < w:
            cat = cat + pltpu.roll(cat, step, 0)
            step *= 2
        win = cat[POOL_HALO:, :]
        cnt = jnp.minimum(row + 1, w).astype(F32)
        pooled = (win / cnt - vg).astype(BF16)
        mixed = jnp.dot(pooled, wg_ref[g], preferred_element_type=F32)
        mixed = mixed * ls_ref[:, sl] * sg_ref[:, sl].astype(F32)
        o_ref[:, sl] = mixed.astype(o_ref.dtype)


def _pool_core(v, sgate, w_grp, layer_scale, tm=256):
    s, d = v.shape
    tm = min(tm, s)
    per = tm // POOL_HALO
    return pl.pallas_call(
        functools.partial(_pool_kernel, tm=tm),
        grid=(s // tm,),
        in_specs=[pl.BlockSpec((tm, d), lambda i: (i, 0)),
                  pl.BlockSpec((POOL_HALO, d), lambda i: (jnp.maximum(i * per - 1, 0), 0)),
                  pl.BlockSpec((tm, d), lambda i: (i, 0)),
                  pl.BlockSpec(w_grp.shape, lambda i: (0, 0, 0)),
                  pl.BlockSpec((1, d), lambda i: (0, 0))],
        out_specs=pl.BlockSpec((tm, d), lambda i: (i, 0)),
        out_shape=jax.ShapeDtypeStruct((s, d), BF16),
        compiler_params=_params(("arbitrary",)),
        name="pool_mix",
    )(v, v, sgate, w_grp, layer_scale)


def _sublane_all(op, x):
    for shift in (4, 2, 1):
        x = op(x, pltpu.roll(x, shift, 0))
    return x


def _fold_groups(reduce_fn, x, chains=4):
    g = x.shape[0]
    chains = min(chains, g)
    return reduce_fn(reduce_fn(x.reshape(g // chains, chains, *x.shape[1:]), axis=0), axis=0)


def _dsa_kernel(qblk_ref, ktile_ref, q_ref, k_ref, vt_ref, sg_ref, qi_ref, wt_ref, kip_ref, o_ref,
                sc_ref, s_ref, p_ref, a_ref, m_ref, l_ref, acc_ref, lo_ref, hi_ref, clo_ref,
                *, tq, tk, ksub, top_k):
    step = pl.program_id(0)
    i = qblk_ref[step]
    j = ktile_ref[step]
    last = (i * tq + tq - 1) // tk
    sub = SUBLANES

    @pl.when(j == 0)
    def _index_and_select():
        qpos = i * tq + lax.broadcasted_iota(jnp.int32, (ksub, tq), 1)

        def score_sub(u, carry):
            rmax, rmin = carry
            k0 = pl.multiple_of(u * ksub, ksub)
            acc = jnp.zeros((ksub, tq), F32)
            lhs = kip_ref[u]
            for p in range(IDX_HEADS // 2):
                rhs = qi_ref[:, p * LANES:(p + 1) * LANES]
                s = lax.dot_general(lhs, rhs, NT_DIMS, preferred_element_type=F32)
                acc = acc + jnp.maximum(s[:ksub], 0.0) * wt_ref[2 * p:2 * p + 1, :]
                acc = acc + jnp.maximum(s[ksub:], 0.0) * wt_ref[2 * p + 1:2 * p + 2, :]
            kpos = u * ksub + lax.broadcasted_iota(jnp.int32, (ksub, tq), 0)
            adm = kpos <= qpos
            sc_ref[pl.ds(k0, ksub), :] = jnp.where(adm, acc, NEG_MASK)
            hi_part = jnp.where(adm, acc, NEG_MASK).reshape(ksub // sub, sub, tq)
            lo_part = jnp.where(adm, acc, -NEG_MASK).reshape(ksub // sub, sub, tq)
            return (jnp.maximum(rmax, _fold_groups(jnp.max, hi_part)),
                    jnp.minimum(rmin, _fold_groups(jnp.min, lo_part)))

        def score_tile(jt, carry):
            for r in range(tk // ksub):
                carry = score_sub(jt * (tk // ksub) + r, carry)
            return carry

        rmax, rmin = lax.fori_loop(
            0, last + 1, score_tile,
            (jnp.full((sub, tq), NEG_MASK, F32), jnp.full((sub, tq), -NEG_MASK, F32)))
        rmax = _sublane_all(jnp.maximum, rmax)
        rmin = _sublane_all(jnp.minimum, rmin)

        n_adm = (i * tq + lax.broadcasted_iota(jnp.int32, (sub, tq), 1) + 1).astype(F32)
        all_keys = n_adm <= float(top_k)
        lo_ref[...] = jnp.where(all_keys, SELECT_ALL, rmin)
        hi_ref[...] = rmax + (jnp.abs(rmax) * 1e-6 + 1e-30)
        clo_ref[...] = jnp.where(all_keys, float(top_k), n_adm)

        def count_ge(thr):
            def body(jt, cnt):
                k0 = pl.multiple_of(jt * tk, tk)
                t = sc_ref[pl.ds(k0, tk), :].reshape(tk // sub, sub, tq)
                return cnt + _fold_groups(jnp.sum, jnp.where(t >= thr[None], 1.0, 0.0))
            cnt = lax.fori_loop(0, last + 1, body, jnp.zeros((sub, tq), F32))
            return _sublane_all(jnp.add, cnt)

        def n_open():
            lo, hi, clo = lo_ref[...], hi_ref[...], clo_ref[...]
            mid = lo + (hi - lo) * 0.5
            is_open = (clo != float(top_k)) & (mid > lo) & (mid < hi)
            return jnp.sum(jnp.where(is_open[0:1, :], 1.0, 0.0))

        def cond(carry):
            it, open_rows = carry
            return (it < MAX_BISECT) & (open_rows > 0.0)

        def body(carry):
            it, _ = carry
            lo, hi, clo = lo_ref[...], hi_ref[...], clo_ref[...]
            mid = lo + (hi - lo) * 0.5
            is_open = (clo != float(top_k)) & (mid > lo) & (mid < hi)
            cnt = count_ge(mid)
            take = cnt >= float(top_k)
            up = is_open & take
            lo_ref[...] = jnp.where(up, mid, lo)
            clo_ref[...] = jnp.where(up, cnt, clo)
            hi_ref[...] = jnp.where(is_open & jnp.logical_not(take), mid, hi)
            return it + 1, n_open()

        lax.while_loop(cond, body, (jnp.int32(0), n_open()))

        thr = lo_ref[...]

        def to_bias(jt, _):
            k0 = pl.multiple_of(jt * tk, tk)
            t = sc_ref[pl.ds(k0, tk), :].reshape(tk // sub, sub, tq)
            sc_ref[pl.ds(k0, tk), :] = jnp.where(t >= thr[None], 0.0, NEG_MASK).reshape(tk, tq)
            return 0

        lax.fori_loop(0, last + 1, to_bias, 0)

        m_ref[...] = jnp.full(m_ref.shape, NEG_MASK, F32)
        l_ref[...] = jnp.zeros(l_ref.shape, F32)
        acc_ref[...] = jnp.zeros(acc_ref.shape, F32)

    def _attend():
        kbase = pl.multiple_of(j * tk, tk)

        def logits(h, slot):
            s_ref[slot] = lax.dot_general(k_ref[h], q_ref[h], NT_DIMS, preferred_element_type=F32)

        def softmax(h, slot):
            x = (s_ref[slot] + sc_ref[pl.ds(kbase, tk), :]).reshape(tk // sub, sub, tq)
            m_prev = m_ref[h]
            m_new = jnp.maximum(m_prev, _sublane_all(jnp.maximum, _fold_groups(jnp.max, x)))
            alpha = jnp.exp2(m_prev - m_new)
            p = jnp.exp2(x - m_new[None])
            p_ref[slot] = p.reshape(tk, tq).astype(BF16)
            a_ref[slot] = alpha
            m_ref[h] = m_new

        ones_rows = jnp.ones((2 * sub, tk), BF16)

        def weighted_values(h, slot):
            lhs = jnp.concatenate([vt_ref[h], ones_rows], axis=0)
            res = jnp.dot(lhs, p_ref[slot], preferred_element_type=F32)
            alpha = a_ref[slot]
            acc = acc_ref[h].reshape(HEAD_DIM // sub, sub, tq) * alpha[None]
            acc_ref[h] = acc.reshape(HEAD_DIM, tq) + res[:HEAD_DIM]
            l_ref[h] = alpha * l_ref[h] + res[HEAD_DIM:HEAD_DIM + sub]

        logits(0, 0)
        logits(1, 1)
        softmax(0, 0)

        for h in range(N_HEADS - 2):
            logits(h + 2, h % 2)
            softmax(h + 1, (h + 1) % 2)
            weighted_values(h, h % 2)
        softmax(N_HEADS - 1, 1)
        weighted_values(N_HEADS - 2, 0)
        weighted_values(N_HEADS - 1, 1)

    _attend()

    @pl.when(j == last)
    def _finish():
        for h in range(N_HEADS):
            hs = slice(h * HEAD_DIM, (h + 1) * HEAD_DIM)
            l = l_ref[h]
            o_t = (acc_ref[h].reshape(HEAD_DIM // sub, sub, tq) / l[None]).reshape(HEAD_DIM, tq)
            o_ref[:, hs] = (o_t.T * sg_ref[:, hs].astype(F32)).astype(o_ref.dtype)


def _dsa_attention(qk, vt, sg, qi, wt, kip, *, tq=256, tk=512, ksub=128):
    s = sg.shape[0]
    tq = min(tq, s)
    tk = min(tk, s)
    top_k = min(TOP_K_MAX, s // 4)
    aw = ATTN_WIDTH

    pairs = [(i, j) for i in range(s // tq) for j in range((i * tq + tq - 1) // tk + 1)]
    qblk = jnp.asarray([p[0] for p in pairs], jnp.int32)
    ktile = jnp.asarray([p[1] for p in pairs], jnp.int32)

    grid_spec = pltpu.PrefetchScalarGridSpec(
        num_scalar_prefetch=2,
        grid=(len(pairs),),
        in_specs=[
            pl.BlockSpec((N_HEADS, tq, HEAD_DIM), lambda t, qb, kt: (0, qb[t], 0)),
            pl.BlockSpec((N_HEADS, tk, HEAD_DIM), lambda t, qb, kt: (1, kt[t], 0)),
            pl.BlockSpec((N_HEADS, HEAD_DIM, tk), lambda t, qb, kt: (0, 0, kt[t])),
            pl.BlockSpec((tq, aw), lambda t, qb, kt: (qb[t], 0)),
            pl.BlockSpec((tq, IDX_HEADS * IDX_DIM), lambda t, qb, kt: (qb[t], 0)),
            pl.BlockSpec((IDX_HEADS, tq), lambda t, qb, kt: (0, qb[t])),
            pl.BlockSpec(kip.shape, lambda t, qb, kt: (0, 0, 0)),
        ],
        out_specs=pl.BlockSpec((tq, aw), lambda t, qb, kt: (qb[t], 0)),
        scratch_shapes=[
            pltpu.VMEM((s, tq), F32),
            pltpu.VMEM((2, tk, tq), F32),
            pltpu.VMEM((2, tk, tq), BF16),
            pltpu.VMEM((2, SUBLANES, tq), F32),
            pltpu.VMEM((N_HEADS, SUBLANES, tq), F32),
            pltpu.VMEM((N_HEADS, SUBLANES, tq), F32),
            pltpu.VMEM((N_HEADS, HEAD_DIM, tq), F32),
            pltpu.VMEM((SUBLANES, tq), F32),
            pltpu.VMEM((SUBLANES, tq), F32),
            pltpu.VMEM((SUBLANES, tq), F32),
        ],
    )
    return pl.pallas_call(
        functools.partial(_dsa_kernel, tq=tq, tk=tk, ksub=ksub, top_k=top_k),
        grid_spec=grid_spec,
        out_shape=jax.ShapeDtypeStruct((s, aw), BF16),
        compiler_params=_params(("arbitrary",)),
        name="dsa_index_select_attend",
    )(qblk, ktile, qk, qk, vt, sg, qi, wt, kip)


def _pool_layer(x, h, gate_mod, w_in, w_grp, layer_scale, w_out):
    d = D_MODEL
    w_in = w_in.astype(BF16)
    v = _mm(h, w_in[:, :d], epilogue="plain", out_dtype=F32, name="pool_in_value")
    sgate = _mm(h, w_in[:, d:], epilogue="silu", out_dtype=BF16, name="pool_in_gate")
    mixed = _pool_core(v, sgate, w_grp.astype(BF16), layer_scale.reshape(1, d))
    return _mm(mixed, w_out.astype(BF16), epilogue="resid", out_dtype=F32, x=x, gm=gate_mod,
               name="pool_out_residual")


def _dsa_layer(x, h, gate_mod, w_in, w_out, tabs):
    aw = ATTN_WIDTH
    nqi = IDX_HEADS * IDX_DIM
    w_qk = w_in[:, :2 * aw].astype(BF16)
    w_vt = w_in[:, 2 * aw:3 * aw].T.astype(BF16)
    w_g = w_in[:, 3 * aw:4 * aw].astype(BF16)
    n_idx = nqi + LANES
    w_idx = jnp.pad(w_in[:, 4 * aw:], ((0, 0), (0, n_idx - (w_in.shape[1] - 4 * aw)))).astype(BF16)

    tn = 1024
    qk = _mm_rope_heads(h, w_qk, tabs, tab_base=0, tab_split=aw // tn, half=ROT_DIM // 2, tn=tn,
                        name="dsa_in_qk")
    vt = _mm_heads_t(w_vt, h, tm=1024, name="dsa_in_vt")
    sg = _mm(h, w_g, epilogue="silu", out_dtype=BF16, name="dsa_in_gate")
    qi, kw = _mm_idx(h, w_idx, tabs, tab_pair=1, half=IDX_ROT // 2)

    ki = kw[:, :IDX_DIM].astype(BF16)
    zeros = jnp.zeros_like(ki)
    ksub = 128
    nsub = ki.shape[0] // ksub
    kip = jnp.concatenate([jnp.concatenate([ki, zeros], axis=1).reshape(nsub, ksub, LANES),
                           jnp.concatenate([zeros, ki], axis=1).reshape(nsub, ksub, LANES)],
                          axis=1)
    wt = kw[:, IDX_DIM:IDX_DIM + IDX_HEADS].T
    og = _dsa_attention(qk, vt, sg, qi, wt, kip, ksub=ksub)
    return _mm(og, w_out.astype(BF16), epilogue="resid", out_dtype=F32, x=x, gm=gate_mod,
               name="dsa_out_residual")


@jax.jit
def kernel(x, c, positions, norm_g, mod_w, mod_b, pool_w_in, pool_w_grp, pool_scale, pool_w_out,
           attn_w_in, attn_w_out, final_g):
    b, s, d = x.shape
    assert b == 1 and d == D_MODEL
    xs = x[0]
    mod = _modulation(c, mod_w, mod_b)

    attn_scale = HEAD_DIM ** -0.5
    idx_scale = (IDX_HEADS ** -0.5) * (IDX_DIM ** -0.5)
    q_fold = attn_scale * math.log2(math.e)
    consts = jnp.stack([
        _rope_consts(HEAD_DIM, ROT_DIM, q_fold, LANES, 1.0, 0, 0),
        _rope_consts(HEAD_DIM, ROT_DIM, 1.0, LANES, 1.0, 0, 0),
        _rope_consts(IDX_DIM, IDX_ROT, 1.0, LANES, 1.0, 0, 0),
        _rope_consts(IDX_DIM, IDX_ROT, 1.0, IDX_DIM, idx_scale, IDX_DIM, IDX_DIM + IDX_HEADS),
    ])
    tabs = _rope_tables(positions[0], consts)

    zero_row = jnp.zeros((1, d), F32)
    for i in range(DEPTH):
        shift, scale, gate = mod[i, :, :d], mod[i, :, d:2 * d], mod[i, :, 2 * d:]
        h = _norm(xs, norm_g[i].reshape(1, d), scale, shift, modulate=True, out_dtype=BF16)
        jl = i // 2
        if i % 2 == 0:
            xs = _pool_layer(xs, h, gate, pool_w_in[jl], pool_w_grp[jl], pool_scale[jl], pool_w_out[jl])
        else:
            xs = _dsa_layer(xs, h, gate, attn_w_in[jl], attn_w_out[jl], tabs)
    out = _norm(xs, final_g.reshape(1, d), zero_row, zero_row, modulate=False, out_dtype=F32)
    return out[None]
```

```python
import functools
import math

import jax
import jax.numpy as jnp
from jax import lax
from jax.experimental import pallas as pl
from jax.experimental.pallas import tpu as pltpu

F32 = jnp.float32
BF16 = jnp.bfloat16

D_MODEL = 2048
DEPTH = 4
EPS = 1e-6
POOL_WINDOWS = (2, 4, 8, 16)
POOL_GROUP_DIM = D_MODEL // len(POOL_WINDOWS)
POOL_HALO = 16
N_HEADS = 16
HEAD_DIM = 128
ATTN_WIDTH = N_HEADS * HEAD_DIM
ROT_DIM = HEAD_DIM // 4
IDX_HEADS = 16
IDX_DIM = 64
IDX_ROT = IDX_DIM // 4
ROPE_THETA = 500000.0
TOP_K_MAX = 256
LANES = 128
SUBLANES = 8
VMEM_LIMIT = 56 * 1024 * 1024

NEG_MASK = -1e30
SELECT_ALL = -1e29
MAX_BISECT = 64
LOGITS_RING = 3
PROBS_RING = 2
NT_DIMS = (((1,), (1,)), ((), ()))


def _params(sem):
    return pltpu.CompilerParams(dimension_semantics=sem, vmem_limit_bytes=VMEM_LIMIT)


def _mod_kernel(c_ref, w_ref, b_ref, o_ref, cond_ref, *, kc):
    @pl.when((pl.program_id(0) == 0) & (pl.program_id(1) == 0))
    def _():
        cc = c_ref[...]
        cond_ref[...] = jnp.broadcast_to(cc * jax.nn.sigmoid(cc), cond_ref.shape)

    tn = o_ref.shape[-1]

    def body(i, acc):
        k0 = pl.multiple_of(i * kc, kc)
        wk = w_ref[0, pl.ds(k0, kc), :]
        ck = jnp.tile(cond_ref[pl.ds(k0, kc), :], (1, tn // LANES))
        return acc + jnp.sum((wk * ck).reshape(kc // 8, 8, tn), axis=0)

    acc = lax.fori_loop(0, D_MODEL // kc, body, jnp.zeros((8, tn), F32))
    o_ref[0] = jnp.sum(acc, axis=0, keepdims=True) + b_ref[0]


def _modulation(c, mod_w, mod_b):
    depth, d, n = mod_w.shape
    tn = 512
    out = pl.pallas_call(
        functools.partial(_mod_kernel, kc=64),
        grid=(depth, n // tn),
        in_specs=[
            pl.BlockSpec((d, 1), lambda l, j: (0, 0)),
            pl.BlockSpec((1, d, tn), lambda l, j: (l, 0, j)),
            pl.BlockSpec((1, 1, tn), lambda l, j: (l, 0, j)),
        ],
        out_specs=pl.BlockSpec((1, 1, tn), lambda l, j: (l, 0, j)),
        out_shape=jax.ShapeDtypeStruct((depth, 1, n), F32),
        scratch_shapes=[pltpu.VMEM((d, LANES), F32)],
        compiler_params=_params(("arbitrary", "arbitrary")),
        name="adaln_modulation",
    )(c.reshape(d, 1), mod_w, mod_b.reshape(depth, 1, n))
    return out


def _norm_kernel(x_ref, g_ref, sc_ref, sh_ref, o_ref, *, modulate):
    x = x_ref[...]
    ms = jnp.mean(x * x, axis=-1, keepdims=True)
    y = x * lax.rsqrt(ms + EPS) * g_ref[...]
    if modulate:
        y = y * (1.0 + sc_ref[...]) + sh_ref[...]
    o_ref[...] = y.astype(o_ref.dtype)


def _norm(x, g, scale, shift, *, modulate, out_dtype, tm=512):
    s, d = x.shape
    row = pl.BlockSpec((1, d), lambda i: (0, 0))
    return pl.pallas_call(
        functools.partial(_norm_kernel, modulate=modulate),
        grid=(s // tm,),
        in_specs=[pl.BlockSpec((tm, d), lambda i: (i, 0)), row, row, row],
        out_specs=pl.BlockSpec((tm, d), lambda i: (i, 0)),
        out_shape=jax.ShapeDtypeStruct((s, d), out_dtype),
        compiler_params=_params(("arbitrary",)),
        name="rmsnorm_modulate" if modulate else "rmsnorm_final",
    )(x, g, scale, shift)


def _rope_apply(acc, tab, half):
    tn = acc.shape[1]
    reps = tn // LANES
    c = jnp.tile(tab[:, :LANES], (1, reps))
    s1 = jnp.tile(tab[:, LANES:2 * LANES], (1, reps))
    s2 = jnp.tile(tab[:, 2 * LANES:], (1, reps))
    return (acc * c + pltpu.roll(acc, half, 1) * s1 + pltpu.roll(acc, tn - half, 1) * s2)


def _mm_plain_kernel(a_ref, w_ref, o_ref):
    acc = jnp.dot(a_ref[...], w_ref[...], preferred_element_type=F32)
    o_ref[...] = acc.astype(o_ref.dtype)


def _mm_silu_kernel(a_ref, w_ref, o_ref):
    acc = jnp.dot(a_ref[...], w_ref[...], preferred_element_type=F32)
    o_ref[...] = (acc * jax.nn.sigmoid(acc)).astype(o_ref.dtype)


def _mm_rope_heads_kernel(a_ref, w_ref, tab_ref, o_ref, *, half):
    a = a_ref[...]
    tab = tab_ref[0]
    for c in range(o_ref.shape[0] // 2):
        acc = jnp.dot(a, w_ref[:, 2 * c * HEAD_DIM:2 * (c + 1) * HEAD_DIM], preferred_element_type=F32)
        out = _rope_apply(acc, tab, half).astype(o_ref.dtype)
        o_ref[2 * c] = out[:, :HEAD_DIM]
        o_ref[2 * c + 1] = out[:, HEAD_DIM:]


def _mm_heads_t_kernel(wt_ref, a_ref, o_ref):
    res = lax.dot_general(wt_ref[...], a_ref[...], NT_DIMS, preferred_element_type=F32)
    o_ref[...] = res.reshape(o_ref.shape).astype(o_ref.dtype)


def _mm_resid_norm_kernel(a_ref, w_ref, x_ref, gm_ref, g_ref, sc_ref, sh_ref, *out_refs,
                          modulate, emit_x, chunk):
    if emit_x:
        xo_ref, ho_ref = out_refs
    else:
        ho_ref, xo_ref = out_refs
    a = a_ref[...]
    d = x_ref.shape[1]
    ssq = jnp.zeros((x_ref.shape[0], 1), F32)
    for c in range(d // chunk):
        cs = slice(c * chunk, (c + 1) * chunk)
        acc = jnp.dot(a, w_ref[:, cs], preferred_element_type=F32)
        xn = x_ref[:, cs] + gm_ref[:, cs] * acc
        xo_ref[:, cs] = xn
        ssq = ssq + jnp.sum(xn * xn, axis=-1, keepdims=True)
    y = xo_ref[...] * lax.rsqrt(ssq / d + EPS) * g_ref[...]
    if modulate:
        y = y * (1.0 + sc_ref[...]) + sh_ref[...]
    ho_ref[...] = y.astype(ho_ref.dtype)


def _mm_idx_kernel(a_ref, w_ref, tab_ref, oq_ref, okw_ref, *, half):
    acc = jnp.dot(a_ref[...], w_ref[...], preferred_element_type=F32)
    nq = oq_ref.shape[1]
    groups = nq // LANES
    tab_q = tab_ref[0]
    tab_kw = tab_ref[1]
    c = jnp.concatenate([tab_q[:, :LANES]] * groups + [tab_kw[:, :LANES]], axis=1)
    s1 = jnp.concatenate([tab_q[:, LANES:2 * LANES]] * groups + [tab_kw[:, LANES:2 * LANES]], axis=1)
    s2 = jnp.concatenate([tab_q[:, 2 * LANES:]] * groups + [tab_kw[:, 2 * LANES:]], axis=1)
    tn = acc.shape[1]
    out = acc * c + pltpu.roll(acc, half, 1) * s1 + pltpu.roll(acc, tn - half, 1) * s2
    oq_ref[...] = out[:, :nq].astype(oq_ref.dtype)
    okw_ref[...] = out[:, nq:]


def _mm(a, w, *, epilogue, out_dtype, tm=1024, tn=1024, name="mm"):
    s, k = a.shape
    n = w.shape[1]
    tm = min(tm, s)
    tn = min(tn, n)
    kern = {"plain": _mm_plain_kernel, "silu": _mm_silu_kernel}[epilogue]
    return pl.pallas_call(
        kern, grid=(s // tm, n // tn),
        in_specs=[pl.BlockSpec((tm, k), lambda i, j: (i, 0)), pl.BlockSpec((k, tn), lambda i, j: (0, j))],
        out_specs=pl.BlockSpec((tm, tn), lambda i, j: (i, j)),
        out_shape=jax.ShapeDtypeStruct((s, n), out_dtype),
        compiler_params=_params(("arbitrary", "arbitrary")), name=name,
    )(a, w)


def _mm_resid_norm(a, w, x, gm, g, scale, shift, *, modulate, emit_x, out_dtype, tm=512, chunk=512,
                   name="mm_resid_norm"):
    s, k = a.shape
    d = w.shape[1]
    tm = min(tm, s)
    row = pl.BlockSpec((1, d), lambda i: (0, 0))
    blk = pl.BlockSpec((tm, d), lambda i: (i, 0))
    normed = jax.ShapeDtypeStruct((s, d), out_dtype)
    out = pl.pallas_call(
        functools.partial(_mm_resid_norm_kernel, modulate=modulate, emit_x=emit_x, chunk=chunk),
        grid=(s // tm,),
        in_specs=[pl.BlockSpec((tm, k), lambda i: (i, 0)), pl.BlockSpec((k, d), lambda i: (0, 0)),
                  blk, row, row, row, row],
        out_specs=[blk, blk] if emit_x else blk,
        out_shape=[jax.ShapeDtypeStruct((s, d), F32), normed] if emit_x else normed,
        scratch_shapes=[] if emit_x else [pltpu.VMEM((tm, d), F32)],
        compiler_params=_params(("arbitrary",)),
        name=name,
    )(a, w, x, gm, g, scale, shift)
    return out


def _mm_rope_heads(a, w, tab, *, tab_base, tab_split, half, tm=1024, tn=512, name="mm_rope"):
    s, k = a.shape
    n = w.shape[1]
    tm = min(tm, s)
    hpt = tn // HEAD_DIM
    return pl.pallas_call(
        functools.partial(_mm_rope_heads_kernel, half=half),
        grid=(s // tm, n // tn),
        in_specs=[pl.BlockSpec((tm, k), lambda i, j: (i, 0)),
                  pl.BlockSpec((k, tn), lambda i, j: (0, j)),
                  pl.BlockSpec((1, tm, 3 * LANES), lambda i, j: (tab_base + j // tab_split, i, 0))],
        out_specs=pl.BlockSpec((hpt, tm, HEAD_DIM), lambda i, j: (j, i, 0)),
        out_shape=jax.ShapeDtypeStruct((n // HEAD_DIM, s, HEAD_DIM), BF16),
        compiler_params=_params(("arbitrary", "arbitrary")), name=name,
    )(a, w, tab)


def _mm_heads_t(wt, a, *, tm=512, tn=512, name="mm_heads_t"):
    s, k = a.shape
    n = wt.shape[0]
    tm = min(tm, s)
    hpt = tn // HEAD_DIM
    return pl.pallas_call(
        _mm_heads_t_kernel,
        grid=(s // tm, n // tn),
        in_specs=[pl.BlockSpec((tn, k), lambda i, j: (j, 0)),
                  pl.BlockSpec((tm, k), lambda i, j: (i, 0))],
        out_specs=pl.BlockSpec((hpt, HEAD_DIM, tm), lambda i, j: (j, 0, i)),
        out_shape=jax.ShapeDtypeStruct((n // HEAD_DIM, HEAD_DIM, s), BF16),
        compiler_params=_params(("arbitrary", "arbitrary")), name=name,
    )(wt, a)


def _mm_idx(a, w, tab, *, tab_pair, half, tm=512):
    s, k = a.shape
    n = w.shape[1]
    nq = IDX_HEADS * IDX_DIM
    tm = min(tm, s)
    return pl.pallas_call(
        functools.partial(_mm_idx_kernel, half=half),
        grid=(s // tm,),
        in_specs=[pl.BlockSpec((tm, k), lambda i: (i, 0)),
                  pl.BlockSpec((k, n), lambda i: (0, 0)),
                  pl.BlockSpec((2, tm, 3 * LANES), lambda i: (tab_pair, i, 0))],
        out_specs=[pl.BlockSpec((tm, nq), lambda i: (i, 0)),
                   pl.BlockSpec((tm, LANES), lambda i: (i, 0))],
        out_shape=[jax.ShapeDtypeStruct((s, nq), BF16),
                   jax.ShapeDtypeStruct((s, LANES), F32)],
        compiler_params=_params(("arbitrary",)),
        name="indexer_projection",
    )(a, w, tab)


def _rope_table_kernel(pos_ref, cst_ref, o_ref):
    pos = pos_ref[...].astype(F32)
    for v in range(o_ref.shape[0]):
        cst = cst_ref[v]
        if v % 2 == 0:
            ang = pos * cst[0:1, :]
            cos = jnp.cos(ang)
            sin = jnp.sin(ang)
        first, second, base, scale = cst[1:2, :], cst[2:3, :], cst[3:4, :], cst[4:5, :]
        rot = first + second
        o_ref[v, :, 0:LANES] = (rot * cos + (1.0 - rot) * base) * scale
        o_ref[v, :, LANES:2 * LANES] = second * sin * scale
        o_ref[v, :, 2 * LANES:3 * LANES] = -(first * sin) * scale


def _rope_tables(positions, consts, tm=512):
    s = positions.shape[0]
    nv = consts.shape[0]
    tm = min(tm, s)
    return pl.pallas_call(
        _rope_table_kernel,
        grid=(s // tm,),
        in_specs=[pl.BlockSpec((tm, 1), lambda i: (i, 0)),
                  pl.BlockSpec((nv, 8, LANES), lambda i: (0, 0, 0))],
        out_specs=pl.BlockSpec((nv, tm, 3 * LANES), lambda i: (0, i, 0)),
        out_shape=jax.ShapeDtypeStruct((nv, s, 3 * LANES), F32),
        compiler_params=_params(("arbitrary",)),
        name="rope_tables",
    )(positions.reshape(s, 1), consts)


def _rope_consts(head_dim, rot_dim, scale, n_rot_lanes, base_val, base_lo, base_hi):
    half = rot_dim // 2
    inv_freq = ROPE_THETA ** (-jnp.arange(half, dtype=F32) * 2.0 / rot_dim)
    lane = jnp.arange(LANES)
    inner = lane % head_dim
    active = lane < n_rot_lanes
    first = (inner < half) & active
    second = (inner >= half) & (inner < rot_dim) & active
    invf = jnp.where(first | second, inv_freq[inner % half], 0.0)
    base = jnp.where((lane >= base_lo) & (lane < base_hi), base_val, 1.0)
    rows = [invf, first.astype(F32), second.astype(F32), base.astype(F32),
            jnp.full((LANES,), scale, F32)]
    rows += [jnp.zeros((LANES,), F32)] * (8 - len(rows))
    return jnp.stack(rows).astype(F32)


def _pool_kernel(v_ref, halo_ref, sg_ref, wg_ref, ls_ref, o_ref, *, tm):
    i = pl.program_id(0)
    keep = (i > 0).astype(F32)
    row = i * tm + lax.broadcasted_iota(jnp.int32, (tm, 1), 0)
    gd = POOL_GROUP_DIM
    for g, w in enumerate(POOL_WINDOWS):
        sl = slice(g * gd, (g + 1) * gd)
        vg = v_ref[:, sl]
        cat = jnp.concatenate([halo_ref[:, sl] * keep, vg], axis=0)
        step = 1
        while step < w:
            cat = cat + pltpu.roll(cat, step, 0)
            step *= 2
        win = cat[POOL_HALO:, :]
        cnt = jnp.minimum(row + 1, w).astype(F32)
        pooled = (win / cnt - vg).astype(BF16)
        mixed = jnp.dot(pooled, wg_ref[g], preferred_element_type=F32)
        mixed = mixed * ls_ref[:, sl] * sg_ref[:, sl].astype(F32)
        o_ref[:, sl] = mixed.astype(o_ref.dtype)


def _pool_core(v, sgate, w_grp, layer_scale, tm=256):
    s, d = v.shape
    tm = min(tm, s)
    per = tm // POOL_HALO
    return pl.pallas_call(
        functools.partial(_pool_kernel, tm=tm),
        grid=(s // tm,),
        in_specs=[pl.BlockSpec((tm, d), lambda i: (i, 0)),
                  pl.BlockSpec((POOL_HALO, d), lambda i: (jnp.maximum(i * per - 1, 0), 0)),
                  pl.BlockSpec((tm, d), lambda i: (i, 0)),
                  pl.BlockSpec(w_grp.shape, lambda i: (0, 0, 0)),
                  pl.BlockSpec((1, d), lambda i: (0, 0))],
        out_specs=pl.BlockSpec((tm, d), lambda i: (i, 0)),
        out_shape=jax.ShapeDtypeStruct((s, d), BF16),
        compiler_params=_params(("arbitrary",)),
        name="pool_mix",
    )(v, v, sgate, w_grp, layer_scale)


def _sublane_all(op, x):
    for shift in (4, 2, 1):
        x = op(x, pltpu.roll(x, shift, 0))
    return x


def _fold_groups(reduce_fn, x, chains=4):
    g = x.shape[0]
    chains = min(chains, g)
    return reduce_fn(reduce_fn(x.reshape(g // chains, chains, *x.shape[1:]), axis=0), axis=0)


def _select_to_bias(sc_ref, lo_ref, hi_ref, clo_ref, rmax, rmin, *, first_query, n_tiles, tk, top_k):
    sub = SUBLANES
    tq = sc_ref.shape[1]
    groups = tk // sub
    kf = float(top_k)

    def tiles_sum(tile_fn):
        def one(jt):
            k0 = pl.multiple_of(jt * tk, tk)
            return tile_fn(sc_ref[pl.ds(k0, tk), :].reshape(groups, sub, tq), k0)

        def two(jp, cnt):
            return cnt + one(2 * jp) + one(2 * jp + 1)

        cnt = lax.fori_loop(0, n_tiles // 2, two, jnp.zeros((sub, tq), F32))
        cnt = lax.cond(n_tiles % 2 == 1, lambda c: c + one(n_tiles - 1), lambda c: c, cnt)
        return _sublane_all(jnp.add, cnt)

    def count(pred):
        return tiles_sum(lambda t, k0: _fold_groups(jnp.sum, jnp.where(pred(t, k0), 1.0, 0.0)))

    n_adm = (first_query + lax.broadcasted_iota(jnp.int32, (sub, tq), 1) + 1).astype(F32)
    all_keys = n_adm <= kf
    lo_ref[...] = jnp.where(all_keys, SELECT_ALL, rmin)
    hi_ref[...] = rmax + (jnp.abs(rmax) * 1e-6 + 1e-30)
    clo_ref[...] = jnp.where(all_keys, kf, n_adm)

    def midpoint():
        lo, hi, clo = lo_ref[...], hi_ref[...], clo_ref[...]
        mid = lo + (hi - lo) * 0.5
        return lo, hi, clo, mid, (clo != kf) & (mid > lo) & (mid < hi)

    def n_open():
        return jnp.sum(jnp.where(midpoint()[4][0:1, :], 1.0, 0.0))

    def cond(carry):
        it, open_rows = carry
        return (it < MAX_BISECT) & (open_rows > 0.0)

    def body(carry):
        it, _ = carry
        lo, hi, clo, mid, is_open = midpoint()
        cnt = count(lambda t, k0: t >= mid[None])
        take = cnt >= kf
        up = is_open & take
        lo_ref[...] = jnp.where(up, mid, lo)
        clo_ref[...] = jnp.where(up, cnt, clo)
        hi_ref[...] = jnp.where(is_open & jnp.logical_not(take), mid, hi)
        return it + 1, n_open()

    lax.while_loop(cond, body, (jnp.int32(0), n_open()))

    thr = lo_ref[...]
    tied = clo_ref[...] != kf
    hi_ref[...] = jnp.full((sub, tq), float(sc_ref.shape[0]), F32)

    def key_index(k0):
        return (k0 + lax.broadcasted_iota(jnp.int32, (groups, sub, tq), 0) * sub
                + lax.broadcasted_iota(jnp.int32, (groups, sub, tq), 1)).astype(F32)

    any_tied = jnp.sum(jnp.where(tied[0:1, :], 1.0, 0.0)) > 0.0

    @pl.when(any_tied)
    def _():
        need = kf - count(lambda t, k0: t > thr[None])

        def step(_, bounds):
            below, keep = bounds
            mid = jnp.floor((below + keep) * 0.5)
            cnt = count(lambda t, k0: (t == thr[None]) & (key_index(k0) <= mid[None]))
            ok = cnt >= need
            return jnp.where(ok, below, mid), jnp.where(ok, mid, keep)

        last_key = (n_tiles * tk - 1).astype(F32)
        steps = max(1, math.ceil(math.log2(sc_ref.shape[0]))) + 1
        _, keep = lax.fori_loop(0, steps, step, (jnp.full((sub, tq), -1.0, F32),
                                                 jnp.zeros((sub, tq), F32) + last_key))
        hi_ref[...] = jnp.where(tied, keep, hi_ref[...])

    keep = hi_ref[...]

    def to_bias(with_ties):
        def tile(jt, _):
            k0 = pl.multiple_of(jt * tk, tk)
            t = sc_ref[pl.ds(k0, tk), :].reshape(groups, sub, tq)
            if with_ties:
                sel = (t > thr[None]) | ((t == thr[None]) & (key_index(k0) <= keep[None]))
            else:
                sel = t >= thr[None]
            sc_ref[pl.ds(k0, tk), :] = jnp.where(sel, 0.0, NEG_MASK).reshape(tk, tq)
            return 0
        return lambda: lax.fori_loop(0, n_tiles, tile, 0)

    lax.cond(any_tied, to_bias(True), to_bias(False))


def _dsa_kernel(qblk_ref, ktile_ref, q_ref, k_ref, vt_ref, sg_ref, qi_ref, wt_ref, kip_ref, o_ref,
                sc_ref, s_ref, p_ref, a_ref, m_ref, l_ref, acc_ref, lo_ref, hi_ref, clo_ref,
                *, tq, tk, ksub, top_k):
    step = pl.program_id(0)
    i = qblk_ref[step]
    j = ktile_ref[step]
    last = (i * tq + tq - 1) // tk
    sub = SUBLANES

    @pl.when(j == 0)
    def _index_and_select():
        qpos = i * tq + lax.broadcasted_iota(jnp.int32, (ksub, tq), 1)

        def score_sub(u, carry):
            rmax, rmin = carry
            k0 = pl.multiple_of(u * ksub, ksub)
            acc = jnp.zeros((ksub, tq), F32)
            lhs = kip_ref[u]
            for p in range(IDX_HEADS // 2):
                rhs = qi_ref[:, p * LANES:(p + 1) * LANES]
                s = lax.dot_general(lhs, rhs, NT_DIMS, preferred_element_type=F32)
                acc = acc + jnp.maximum(s[:ksub], 0.0) * wt_ref[2 * p:2 * p + 1, :]
                acc = acc + jnp.maximum(s[ksub:], 0.0) * wt_ref[2 * p + 1:2 * p + 2, :]
            kpos = u * ksub + lax.broadcasted_iota(jnp.int32, (ksub, tq), 0)
            adm = kpos <= qpos
            sc_ref[pl.ds(k0, ksub), :] = jnp.where(adm, acc, NEG_MASK)
            hi_part = jnp.where(adm, acc, NEG_MASK).reshape(ksub // sub, sub, tq)
            lo_part = jnp.where(adm, acc, -NEG_MASK).reshape(ksub // sub, sub, tq)
            return (jnp.maximum(rmax, _fold_groups(jnp.max, hi_part)),
                    jnp.minimum(rmin, _fold_groups(jnp.min, lo_part)))

        def score_tile(jt, carry):
            for r in range(tk // ksub):
                carry = score_sub(jt * (tk // ksub) + r, carry)
            return carry

        rmax, rmin = lax.fori_loop(
            0, last + 1, score_tile,
            (jnp.full((sub, tq), NEG_MASK, F32), jnp.full((sub, tq), -NEG_MASK, F32)))
        rmax = _sublane_all(jnp.maximum, rmax)
        rmin = _sublane_all(jnp.minimum, rmin)

        _select_to_bias(sc_ref, lo_ref, hi_ref, clo_ref, rmax, rmin,
                        first_query=i * tq, n_tiles=last + 1, tk=tk, top_k=top_k)

        m_ref[...] = jnp.full(m_ref.shape, NEG_MASK, F32)
        l_ref[...] = jnp.zeros(l_ref.shape, F32)
        acc_ref[...] = jnp.zeros(acc_ref.shape, F32)

    def _attend():
        kbase = pl.multiple_of(j * tk, tk)

        def logits(h, slot):
            s_ref[slot] = lax.dot_general(k_ref[h], q_ref[h], NT_DIMS, preferred_element_type=F32)

        def softmax(h, s_slot, slot):
            x = (s_ref[s_slot] + sc_ref[pl.ds(kbase, tk), :]).reshape(tk // sub, sub, tq)
            m_prev = m_ref[h]
            m_new = jnp.maximum(m_prev, _sublane_all(jnp.maximum, _fold_groups(jnp.max, x)))
            alpha = jnp.exp2(m_prev - m_new)
            p = jnp.exp2(x - m_new[None])
            p_ref[slot] = p.reshape(tk, tq).astype(BF16)
            a_ref[slot] = alpha
            m_ref[h] = m_new

        ones_rows = jnp.ones((2 * sub, tk), BF16)

        def weighted_values(h, slot):
            lhs = jnp.concatenate([vt_ref[h], ones_rows], axis=0)
            res = jnp.dot(lhs, p_ref[slot], preferred_element_type=F32)
            alpha = a_ref[slot]
            acc = acc_ref[h].reshape(HEAD_DIM // sub, sub, tq) * alpha[None]
            acc_ref[h] = acc.reshape(HEAD_DIM, tq) + res[:HEAD_DIM]
            l_ref[h] = alpha * l_ref[h] + res[HEAD_DIM:HEAD_DIM + sub]

        n_s, n_p = s_ref.shape[0], p_ref.shape[0]
        ahead_s = n_s - 1
        lag = ahead_s + n_p - 1
        for t in range(N_HEADS + lag):
            if t < N_HEADS:
                logits(t, t % n_s)
            hs_ = t - ahead_s
            if 0 <= hs_ < N_HEADS:
                softmax(hs_, hs_ % n_s, hs_ % n_p)
            hv = t - lag
            if 0 <= hv < N_HEADS:
                weighted_values(hv, hv % n_p)

    _attend()

    @pl.when(j == last)
    def _finish():
        for h in range(N_HEADS):
            hs = slice(h * HEAD_DIM, (h + 1) * HEAD_DIM)
            l = l_ref[h]
            o_t = (acc_ref[h].reshape(HEAD_DIM // sub, sub, tq) / l[None]).reshape(HEAD_DIM, tq)
            o_ref[:, hs] = (o_t.T * sg_ref[:, hs].astype(F32)).astype(o_ref.dtype)


def _dsa_attention(qk, vt, sg, qi, wt, kip, *, tq=256, tk=512, ksub=128):
    s = sg.shape[0]
    tq = min(tq, s)
    tk = min(tk, s)
    top_k = min(TOP_K_MAX, s // 4)
    aw = ATTN_WIDTH

    pairs = [(i, j) for i in range(s // tq) for j in range((i * tq + tq - 1) // tk + 1)]
    qblk = jnp.asarray([p[0] for p in pairs], jnp.int32)
    ktile = jnp.asarray([p[1] for p in pairs], jnp.int32)

    grid_spec = pltpu.PrefetchScalarGridSpec(
        num_scalar_prefetch=2,
        grid=(len(pairs),),
        in_specs=[
            pl.BlockSpec((N_HEADS, tq, HEAD_DIM), lambda t, qb, kt: (0, qb[t], 0)),
            pl.BlockSpec((N_HEADS, tk, HEAD_DIM), lambda t, qb, kt: (1, kt[t], 0)),
            pl.BlockSpec((N_HEADS, HEAD_DIM, tk), lambda t, qb, kt: (0, 0, kt[t])),
            pl.BlockSpec((tq, aw), lambda t, qb, kt: (qb[t], 0)),
            pl.BlockSpec((tq, IDX_HEADS * IDX_DIM), lambda t, qb, kt: (qb[t], 0)),
            pl.BlockSpec((IDX_HEADS, tq), lambda t, qb, kt: (0, qb[t])),
            pl.BlockSpec(kip.shape, lambda t, qb, kt: (0, 0, 0)),
        ],
        out_specs=pl.BlockSpec((tq, aw), lambda t, qb, kt: (qb[t], 0)),
        scratch_shapes=[
            pltpu.VMEM((s, tq), F32),
            pltpu.VMEM((LOGITS_RING, tk, tq), F32),
            pltpu.VMEM((PROBS_RING, tk, tq), BF16),
            pltpu.VMEM((PROBS_RING, SUBLANES, tq), F32),
            pltpu.VMEM((N_HEADS, SUBLANES, tq), F32),
            pltpu.VMEM((N_HEADS, SUBLANES, tq), F32),
            pltpu.VMEM((N_HEADS, HEAD_DIM, tq), F32),
            pltpu.VMEM((SUBLANES, tq), F32),
            pltpu.VMEM((SUBLANES, tq), F32),
            pltpu.VMEM((SUBLANES, tq), F32),
        ],
    )
    return pl.pallas_call(
        functools.partial(_dsa_kernel, tq=tq, tk=tk, ksub=ksub, top_k=top_k),
        grid_spec=grid_spec,
        out_shape=jax.ShapeDtypeStruct((s, aw), BF16),
        compiler_params=_params(("arbitrary",)),
        name="dsa_index_select_attend",
    )(qblk, ktile, qk, qk, vt, sg, qi, wt, kip)


def _pool_mixer(h, w_in, w_grp, layer_scale):
    d = D_MODEL
    w_in = w_in.astype(BF16)
    v = _mm(h, w_in[:, :d], epilogue="plain", out_dtype=F32, name="pool_in_value")
    sgate = _mm(h, w_in[:, d:], epilogue="silu", out_dtype=BF16, name="pool_in_gate")
    return _pool_core(v, sgate, w_grp.astype(BF16), layer_scale.reshape(1, d))


def _dsa_mixer(h, w_in, tabs):
    aw = ATTN_WIDTH
    nqi = IDX_HEADS * IDX_DIM
    w_qk = w_in[:, :2 * aw].astype(BF16)
    w_vt = w_in[:, 2 * aw:3 * aw].T.astype(BF16)
    w_g = w_in[:, 3 * aw:4 * aw].astype(BF16)
    n_idx = nqi + LANES
    w_idx = jnp.pad(w_in[:, 4 * aw:], ((0, 0), (0, n_idx - (w_in.shape[1] - 4 * aw)))).astype(BF16)

    tn = 1024
    qk = _mm_rope_heads(h, w_qk, tabs, tab_base=0, tab_split=aw // tn, half=ROT_DIM // 2, tn=tn,
                        name="dsa_in_qk")
    vt = _mm_heads_t(w_vt, h, tm=1024, name="dsa_in_vt")
    sg = _mm(h, w_g, epilogue="silu", out_dtype=BF16, name="dsa_in_gate")
    qi, kw = _mm_idx(h, w_idx, tabs, tab_pair=1, half=IDX_ROT // 2)

    ki = kw[:, :IDX_DIM].astype(BF16)
    zeros = jnp.zeros_like(ki)
    ksub = 128
    nsub = ki.shape[0] // ksub
    kip = jnp.concatenate([jnp.concatenate([ki, zeros], axis=1).reshape(nsub, ksub, LANES),
                           jnp.concatenate([zeros, ki], axis=1).reshape(nsub, ksub, LANES)],
                          axis=1)
    wt = kw[:, IDX_DIM:IDX_DIM + IDX_HEADS].T
    return _dsa_attention(qk, vt, sg, qi, wt, kip, ksub=ksub)


@jax.jit
def kernel(x, c, positions, norm_g, mod_w, mod_b, pool_w_in, pool_w_grp, pool_scale, pool_w_out,
           attn_w_in, attn_w_out, final_g):
    b, s, d = x.shape
    assert b == 1 and d == D_MODEL
    xs = x[0]
    mod = _modulation(c, mod_w, mod_b)

    attn_scale = HEAD_DIM ** -0.5
    idx_scale = (IDX_HEADS ** -0.5) * (IDX_DIM ** -0.5)
    q_fold = attn_scale * math.log2(math.e)
    consts = jnp.stack([
        _rope_consts(HEAD_DIM, ROT_DIM, q_fold, LANES, 1.0, 0, 0),
        _rope_consts(HEAD_DIM, ROT_DIM, 1.0, LANES, 1.0, 0, 0),
        _rope_consts(IDX_DIM, IDX_ROT, 1.0, LANES, 1.0, 0, 0),
        _rope_consts(IDX_DIM, IDX_ROT, 1.0, IDX_DIM, idx_scale, IDX_DIM, IDX_DIM + IDX_HEADS),
    ])
    tabs = _rope_tables(positions[0], consts)

    zero_row = jnp.zeros((1, d), F32)

    def mod_rows(i):
        return mod[i, :, :d], mod[i, :, d:2 * d], mod[i, :, 2 * d:]

    shift, scale, gate = mod_rows(0)
    h = _norm(xs, norm_g[0].reshape(1, d), scale, shift, modulate=True, out_dtype=BF16)
    for i in range(DEPTH):
        jl = i // 2
        if i % 2 == 0:
            branch = _pool_mixer(h, pool_w_in[jl], pool_w_grp[jl], pool_scale[jl])
            w_out = pool_w_out[jl]
        else:
            branch = _dsa_mixer(h, attn_w_in[jl], tabs)
            w_out = attn_w_out[jl]
        if i + 1 < DEPTH:
            shift, scale, next_gate = mod_rows(i + 1)
            xs, h = _mm_resid_norm(branch, w_out.astype(BF16), xs, gate, norm_g[i + 1].reshape(1, d),
                                   scale, shift, modulate=True, emit_x=True, out_dtype=BF16,
                                   name="out_residual_norm")
            gate = next_gate
        else:
            out = _mm_resid_norm(branch, w_out.astype(BF16), xs, gate, final_g.reshape(1, d),
                                 zero_row, zero_row, modulate=False, emit_x=False, out_dtype=F32,
                                 name="out_residual_final_norm")
    return out[None]
```

```python
import functools
import math

import jax
import jax.numpy as jnp
from jax import lax
from jax.experimental import pallas as pl
from jax.experimental.pallas import tpu as pltpu

F32 = jnp.float32
BF16 = jnp.bfloat16

D_MODEL = 2048
DEPTH = 4
EPS = 1e-6
POOL_WINDOWS = (2, 4, 8, 16)
POOL_GROUP_DIM = D_MODEL // len(POOL_WINDOWS)
POOL_HALO = 16
N_HEADS = 16
HEAD_DIM = 128
ATTN_WIDTH = N_HEADS * HEAD_DIM
ROT_DIM = HEAD_DIM // 4
IDX_HEADS = 16
IDX_DIM = 64
IDX_ROT = IDX_DIM // 4
ROPE_THETA = 500000.0
TOP_K_MAX = 256
LANES = 128
SUBLANES = 8
VMEM_LIMIT = 56 * 1024 * 1024

NEG_MASK = -1e30
SELECT_ALL = -1e29
MAX_BISECT = 64
BISECT_STEPS = 12
WALK_STEPS = 6
LOGITS_RING = 3
PROBS_RING = 2
NT_DIMS = (((1,), (1,)), ((), ()))


def _params(sem):
    return pltpu.CompilerParams(dimension_semantics=sem, vmem_limit_bytes=VMEM_LIMIT)


def _mod_kernel(c_ref, w_ref, b_ref, o_ref, cond_ref, *, kc):
    @pl.when((pl.program_id(0) == 0) & (pl.program_id(1) == 0))
    def _():
        cc = c_ref[...]
        cond_ref[...] = jnp.broadcast_to(cc * jax.nn.sigmoid(cc), cond_ref.shape)

    tn = o_ref.shape[-1]

    def body(i, acc):
        k0 = pl.multiple_of(i * kc, kc)
        wk = w_ref[0, pl.ds(k0, kc), :]
        ck = jnp.tile(cond_ref[pl.ds(k0, kc), :], (1, tn // LANES))
        return acc + jnp.sum((wk * ck).reshape(kc // 8, 8, tn), axis=0)

    acc = lax.fori_loop(0, D_MODEL // kc, body, jnp.zeros((8, tn), F32))
    o_ref[0] = jnp.sum(acc, axis=0, keepdims=True) + b_ref[0]


def _modulation(c, mod_w, mod_b):
    depth, d, n = mod_w.shape
    tn = 512
    out = pl.pallas_call(
        functools.partial(_mod_kernel, kc=64),
        grid=(depth, n // tn),
        in_specs=[
            pl.BlockSpec((d, 1), lambda l, j: (0, 0)),
            pl.BlockSpec((1, d, tn), lambda l, j: (l, 0, j)),
            pl.BlockSpec((1, 1, tn), lambda l, j: (l, 0, j)),
        ],
        out_specs=pl.BlockSpec((1, 1, tn), lambda l, j: (l, 0, j)),
        out_shape=jax.ShapeDtypeStruct((depth, 1, n), F32),
        scratch_shapes=[pltpu.VMEM((d, LANES), F32)],
        compiler_params=_params(("arbitrary", "arbitrary")),
        name="adaln_modulation",
    )(c.reshape(d, 1), mod_w, mod_b.reshape(depth, 1, n))
    return out


def _norm_kernel(x_ref, g_ref, sc_ref, sh_ref, o_ref, *, modulate):
    x = x_ref[...]
    ms = jnp.mean(x * x, axis=-1, keepdims=True)
    y = x * lax.rsqrt(ms + EPS) * g_ref[...]
    if modulate:
        y = y * (1.0 + sc_ref[...]) + sh_ref[...]
    o_ref[...] = y.astype(o_ref.dtype)


def _norm(x, g, scale, shift, *, modulate, out_dtype, tm=512):
    s, d = x.shape
    row = pl.BlockSpec((1, d), lambda i: (0, 0))
    return pl.pallas_call(
        functools.partial(_norm_kernel, modulate=modulate),
        grid=(s // tm,),
        in_specs=[pl.BlockSpec((tm, d), lambda i: (i, 0)), row, row, row],
        out_specs=pl.BlockSpec((tm, d), lambda i: (i, 0)),
        out_shape=jax.ShapeDtypeStruct((s, d), out_dtype),
        compiler_params=_params(("arbitrary",)),
        name="rmsnorm_modulate" if modulate else "rmsnorm_final",
    )(x, g, scale, shift)


def _rope_apply(acc, tab, half):
    tn = acc.shape[1]
    reps = tn // LANES
    c = jnp.tile(tab[:, :LANES], (1, reps))
    s1 = jnp.tile(tab[:, LANES:2 * LANES], (1, reps))
    s2 = jnp.tile(tab[:, 2 * LANES:], (1, reps))
    return (acc * c + pltpu.roll(acc, half, 1) * s1 + pltpu.roll(acc, tn - half, 1) * s2)


def _mm_plain_kernel(a_ref, w_ref, o_ref):
    acc = jnp.dot(a_ref[...], w_ref[...].astype(BF16), preferred_element_type=F32)
    o_ref[...] = acc.astype(o_ref.dtype)


def _mm_silu_kernel(a_ref, w_ref, o_ref):
    acc = jnp.dot(a_ref[...], w_ref[...].astype(BF16), preferred_element_type=F32)
    o_ref[...] = (acc * jax.nn.sigmoid(acc)).astype(o_ref.dtype)


def _mm_rope_heads_kernel(a_ref, w_ref, tab_ref, o_ref, *, half):
    a = a_ref[...]
    tab = tab_ref[0]
    for c in range(o_ref.shape[0] // 2):
        w = w_ref[:, 2 * c * HEAD_DIM:2 * (c + 1) * HEAD_DIM].astype(BF16)
        acc = jnp.dot(a, w, preferred_element_type=F32)
        out = _rope_apply(acc, tab, half).astype(o_ref.dtype)
        o_ref[2 * c] = out[:, :HEAD_DIM]
        o_ref[2 * c + 1] = out[:, HEAD_DIM:]


def _mm_heads_t_kernel(wt_ref, a_ref, o_ref):
    res = lax.dot_general(wt_ref[...], a_ref[...], NT_DIMS, preferred_element_type=F32)
    o_ref[...] = res.reshape(o_ref.shape).astype(o_ref.dtype)


def _mm_resid_norm_kernel(a_ref, w_ref, x_ref, gm_ref, g_ref, sc_ref, sh_ref, *out_refs,
                          modulate, emit_x, chunk):
    if emit_x:
        xo_ref, ho_ref = out_refs
    else:
        ho_ref, xo_ref = out_refs
    a = a_ref[...]
    d = x_ref.shape[1]
    ssq = jnp.zeros((x_ref.shape[0], 1), F32)
    for c in range(d // chunk):
        cs = slice(c * chunk, (c + 1) * chunk)
        acc = jnp.dot(a, w_ref[:, cs], preferred_element_type=F32)
        xn = x_ref[:, cs] + gm_ref[:, cs] * acc
        xo_ref[:, cs] = xn
        ssq = ssq + jnp.sum(xn * xn, axis=-1, keepdims=True)
    y = xo_ref[...] * lax.rsqrt(ssq / d + EPS) * g_ref[...]
    if modulate:
        y = y * (1.0 + sc_ref[...]) + sh_ref[...]
    ho_ref[...] = y.astype(ho_ref.dtype)


def _mm_idx_kernel(a_ref, w_ref, tab_ref, oq_ref, okw_ref, *, half):
    acc = jnp.dot(a_ref[...], w_ref[...], preferred_element_type=F32)
    nq = oq_ref.shape[1]
    groups = nq // LANES
    tab_q = tab_ref[0]
    tab_kw = tab_ref[1]
    c = jnp.concatenate([tab_q[:, :LANES]] * groups + [tab_kw[:, :LANES]], axis=1)
    s1 = jnp.concatenate([tab_q[:, LANES:2 * LANES]] * groups + [tab_kw[:, LANES:2 * LANES]], axis=1)
    s2 = jnp.concatenate([tab_q[:, 2 * LANES:]] * groups + [tab_kw[:, 2 * LANES:]], axis=1)
    tn = acc.shape[1]
    out = acc * c + pltpu.roll(acc, half, 1) * s1 + pltpu.roll(acc, tn - half, 1) * s2
    oq_ref[...] = out[:, :nq].astype(oq_ref.dtype)
    okw_ref[...] = out[:, nq:]


def _weight_spec(w_stack, layer, col0, tn):
    assert col0 % tn == 0
    k = w_stack.shape[1]
    return pl.BlockSpec((None, k, tn), lambda i, j: (layer, 0, col0 // tn + j))


def _mm(a, w_stack, layer, col0, n, *, epilogue, out_dtype, tm=1024, tn=1024, name="mm"):
    s, k = a.shape
    tm = min(tm, s)
    kern = {"plain": _mm_plain_kernel, "silu": _mm_silu_kernel}[epilogue]
    return pl.pallas_call(
        kern, grid=(s // tm, n // tn),
        in_specs=[pl.BlockSpec((tm, k), lambda i, j: (i, 0)), _weight_spec(w_stack, layer, col0, tn)],
        out_specs=pl.BlockSpec((tm, tn), lambda i, j: (i, j)),
        out_shape=jax.ShapeDtypeStruct((s, n), out_dtype),
        compiler_params=_params(("arbitrary", "arbitrary")), name=name,
    )(a, w_stack)


def _mm_resid_norm(a, w, x, gm, g, scale, shift, *, modulate, emit_x, out_dtype, tm=512, chunk=512,
                   name="mm_resid_norm"):
    s, k = a.shape
    d = w.shape[1]
    tm = min(tm, s)
    row = pl.BlockSpec((1, d), lambda i: (0, 0))
    blk = pl.BlockSpec((tm, d), lambda i: (i, 0))
    normed = jax.ShapeDtypeStruct((s, d), out_dtype)
    out = pl.pallas_call(
        functools.partial(_mm_resid_norm_kernel, modulate=modulate, emit_x=emit_x, chunk=chunk),
        grid=(s // tm,),
        in_specs=[pl.BlockSpec((tm, k), lambda i: (i, 0)), pl.BlockSpec((k, d), lambda i: (0, 0)),
                  blk, row, row, row, row],
        out_specs=[blk, blk] if emit_x else blk,
        out_shape=[jax.ShapeDtypeStruct((s, d), F32), normed] if emit_x else normed,
        scratch_shapes=[] if emit_x else [pltpu.VMEM((tm, d), F32)],
        compiler_params=_params(("arbitrary",)),
        name=name,
    )(a, w, x, gm, g, scale, shift)
    return out


def _mm_rope_heads(a, w_stack, layer, col0, n, tab, *, tab_base, tab_split, half, tm=1024, tn=1024,
                   name="mm_rope"):
    s, k = a.shape
    tm = min(tm, s)
    hpt = tn // HEAD_DIM
    return pl.pallas_call(
        functools.partial(_mm_rope_heads_kernel, half=half),
        grid=(s // tm, n // tn),
        in_specs=[pl.BlockSpec((tm, k), lambda i, j: (i, 0)),
                  _weight_spec(w_stack, layer, col0, tn),
                  pl.BlockSpec((1, tm, 3 * LANES), lambda i, j: (tab_base + j // tab_split, i, 0))],
        out_specs=pl.BlockSpec((hpt, tm, HEAD_DIM), lambda i, j: (j, i, 0)),
        out_shape=jax.ShapeDtypeStruct((n // HEAD_DIM, s, HEAD_DIM), BF16),
        compiler_params=_params(("arbitrary", "arbitrary")), name=name,
    )(a, w_stack, tab)


def _mm_heads_t(wt, a, *, tm=512, tn=512, name="mm_heads_t"):
    s, k = a.shape
    n = wt.shape[0]
    tm = min(tm, s)
    hpt = tn // HEAD_DIM
    return pl.pallas_call(
        _mm_heads_t_kernel,
        grid=(s // tm, n // tn),
        in_specs=[pl.BlockSpec((tn, k), lambda i, j: (j, 0)),
                  pl.BlockSpec((tm, k), lambda i, j: (i, 0))],
        out_specs=pl.BlockSpec((hpt, HEAD_DIM, tm), lambda i, j: (j, 0, i)),
        out_shape=jax.ShapeDtypeStruct((n // HEAD_DIM, HEAD_DIM, s), BF16),
        compiler_params=_params(("arbitrary", "arbitrary")), name=name,
    )(wt, a)


def _mm_idx(a, w, tab, *, tab_pair, half, tm=512):
    s, k = a.shape
    n = w.shape[1]
    nq = IDX_HEADS * IDX_DIM
    tm = min(tm, s)
    return pl.pallas_call(
        functools.partial(_mm_idx_kernel, half=half),
        grid=(s // tm,),
        in_specs=[pl.BlockSpec((tm, k), lambda i: (i, 0)),
                  pl.BlockSpec((k, n), lambda i: (0, 0)),
                  pl.BlockSpec((2, tm, 3 * LANES), lambda i: (tab_pair, i, 0))],
        out_specs=[pl.BlockSpec((tm, nq), lambda i: (i, 0)),
                   pl.BlockSpec((tm, LANES), lambda i: (i, 0))],
        out_shape=[jax.ShapeDtypeStruct((s, nq), BF16),
                   jax.ShapeDtypeStruct((s, LANES), F32)],
        compiler_params=_params(("arbitrary",)),
        name="indexer_projection",
    )(a, w, tab)


def _rope_table_kernel(pos_ref, cst_ref, o_ref):
    pos = pos_ref[...].astype(F32)
    for v in range(o_ref.shape[0]):
        cst = cst_ref[v]
        if v % 2 == 0:
            ang = pos * cst[0:1, :]
            cos = jnp.cos(ang)
            sin = jnp.sin(ang)
        first, second, base, scale = cst[1:2, :], cst[2:3, :], cst[3:4, :], cst[4:5, :]
        rot = first + second
        o_ref[v, :, 0:LANES] = (rot * cos + (1.0 - rot) * base) * scale
        o_ref[v, :, LANES:2 * LANES] = second * sin * scale
        o_ref[v, :, 2 * LANES:3 * LANES] = -(first * sin) * scale


def _rope_tables(positions, consts, tm=512):
    s = positions.shape[0]
    nv = consts.shape[0]
    tm = min(tm, s)
    return pl.pallas_call(
        _rope_table_kernel,
        grid=(s // tm,),
        in_specs=[pl.BlockSpec((tm, 1), lambda i: (i, 0)),
                  pl.BlockSpec((nv, 8, LANES), lambda i: (0, 0, 0))],
        out_specs=pl.BlockSpec((nv, tm, 3 * LANES), lambda i: (0, i, 0)),
        out_shape=jax.ShapeDtypeStruct((nv, s, 3 * LANES), F32),
        compiler_params=_params(("arbitrary",)),
        name="rope_tables",
    )(positions.reshape(s, 1), consts)


def _rope_consts(head_dim, rot_dim, scale, n_rot_lanes, base_val, base_lo, base_hi):
    half = rot_dim // 2
    inv_freq = ROPE_THETA ** (-jnp.arange(half, dtype=F32) * 2.0 / rot_dim)
    lane = jnp.arange(LANES)
    inner = lane % head_dim
    active = lane < n_rot_lanes
    first = (inner < half) & active
    second = (inner >= half) & (inner < rot_dim) & active
    invf = jnp.where(first | second, inv_freq[inner % half], 0.0)
    base = jnp.where((lane >= base_lo) & (lane < base_hi), base_val, 1.0)
    rows = [invf, first.astype(F32), second.astype(F32), base.astype(F32),
            jnp.full((LANES,), scale, F32)]
    rows += [jnp.zeros((LANES,), F32)] * (8 - len(rows))
    return jnp.stack(rows).astype(F32)


def _pool_kernel(v_ref, halo_ref, sg_ref, wg_ref, ls_ref, o_ref, *, tm):
    i = pl.program_id(0)
    keep = (i > 0).astype(F32)
    row = i * tm + lax.broadcasted_iota(jnp.int32, (tm, 1), 0)
    gd = POOL_GROUP_DIM
    for g, w in enumerate(POOL_WINDOWS):
        sl = slice(g * gd, (g + 1) * gd)
        vg = v_ref[:, sl]
        cat = jnp.concatenate([halo_ref[:, sl] * keep, vg], axis=0)
        step = 1
        while step < w:
            cat = cat + pltpu.roll(cat, step, 0)
            step *= 2
        win = cat[POOL_HALO:, :]
        cnt = jnp.minimum(row + 1, w).astype(F32)
        pooled = (win / cnt - vg).astype(BF16)
        mixed = jnp.dot(pooled, wg_ref[g], preferred_element_type=F32)
        mixed = mixed * ls_ref[:, sl] * sg_ref[:, sl].astype(F32)
        o_ref[:, sl] = mixed.astype(o_ref.dtype)


def _pool_core(v, sgate, w_grp, layer_scale, tm=256):
    s, d = v.shape
    tm = min(tm, s)
    per = tm // POOL_HALO
    return pl.pallas_call(
        functools.partial(_pool_kernel, tm=tm),
        grid=(s // tm,),
        in_specs=[pl.BlockSpec((tm, d), lambda i: (i, 0)),
                  pl.BlockSpec((POOL_HALO, d), lambda i: (jnp.maximum(i * per - 1, 0), 0)),
                  pl.BlockSpec((tm, d), lambda i: (i, 0)),
                  pl.BlockSpec(w_grp.shape, lambda i: (0, 0, 0)),
                  pl.BlockSpec((1, d), lambda i: (0, 0))],
        out_specs=pl.BlockSpec((tm, d), lambda i: (i, 0)),
        out_shape=jax.ShapeDtypeStruct((s, d), BF16),
        compiler_params=_params(("arbitrary",)),
        name="pool_mix",
    )(v, v, sgate, w_grp, layer_scale)


def _sublane_all(op, x):
    for shift in (4, 2, 1):
        x = op(x, pltpu.roll(x, shift, 0))
    return x


def _fold_groups(reduce_fn, x, chains=4):
    g = x.shape[0]
    chains = min(chains, g)
    return reduce_fn(reduce_fn(x.reshape(g // chains, chains, *x.shape[1:]), axis=0), axis=0)


def _select_to_bias(sc_ref, lo_ref, hi_ref, clo_ref, chi_ref, rmax, rmin, *, first_query, n_tiles, tk, top_k):
    sub = SUBLANES
    tq = sc_ref.shape[1]
    groups = tk // sub
    kf = float(top_k)

    def tiles_reduce(tile_fn, combine, init):
        def one(jt):
            k0 = pl.multiple_of(jt * tk, tk)
            return tile_fn(sc_ref[pl.ds(k0, tk), :].reshape(groups, sub, tq), k0)

        def two(jp, acc):
            return combine(combine(acc, one(2 * jp)), one(2 * jp + 1))

        acc = lax.fori_loop(0, n_tiles // 2, two, jnp.full((sub, tq), init, F32))
        acc = lax.cond(n_tiles % 2 == 1, lambda c: combine(c, one(n_tiles - 1)), lambda c: c, acc)
        return _sublane_all(combine, acc)

    def count(pred):
        return tiles_reduce(lambda t, k0: _fold_groups(jnp.sum, jnp.where(pred(t, k0), 1.0, 0.0)),
                            jnp.add, 0.0)

    def any_lane(flags):
        return jnp.sum(jnp.where(flags[0:1, :], 1.0, 0.0)) > 0.0

    n_adm = (first_query + lax.broadcasted_iota(jnp.int32, (sub, tq), 1) + 1).astype(F32)
    all_keys = n_adm <= kf
    lo_ref[...] = jnp.where(all_keys, SELECT_ALL, rmin)
    hi_ref[...] = rmax + (jnp.abs(rmax) * 1e-6 + 1e-30)
    clo_ref[...] = jnp.where(all_keys, kf, n_adm)
    chi_ref[...] = jnp.zeros((sub, tq), F32)

    def midpoint():
        lo, hi, clo = lo_ref[...], hi_ref[...], clo_ref[...]
        mid = lo + (hi - lo) * 0.5
        return lo, hi, clo, mid, (clo != kf) & (mid > lo) & (mid < hi)

    def bisect(max_steps):
        def cond(carry):
            it, go = carry
            return (it < max_steps) & go

        def body(carry):
            it, _ = carry
            lo, hi, clo, mid, is_open = midpoint()
            cnt = count(lambda t, k0: t >= mid[None])
            take = cnt >= kf
            up = is_open & take
            down = is_open & jnp.logical_not(take)
            lo_ref[...] = jnp.where(up, mid, lo)
            clo_ref[...] = jnp.where(up, cnt, clo)
            hi_ref[...] = jnp.where(down, mid, hi)
            chi_ref[...] = jnp.where(down, cnt, chi_ref[...])
            return it + 1, any_lane(midpoint()[4])

        lax.while_loop(cond, body, (jnp.int32(0), any_lane(midpoint()[4])))

    bisect(BISECT_STEPS)
    hi_bisect = hi_ref[...]

    def walking():
        return (clo_ref[...] != kf) & (chi_ref[...] < kf)

    def walk_cond(carry):
        it, go = carry
        return (it < WALK_STEPS) & go

    def walk_body(carry):
        it, _ = carry
        hi = hi_ref[...]
        go = walking()
        nxt = tiles_reduce(lambda t, k0: _fold_groups(jnp.max, jnp.where(t < hi[None], t, NEG_MASK)),
                           jnp.maximum, NEG_MASK)
        hi_ref[...] = jnp.where(go, nxt, hi)
        chi_ref[...] = jnp.where(go, chi_ref[...] + 1.0, chi_ref[...])
        return it + 1, any_lane(walking())

    lax.while_loop(walk_cond, walk_body, (jnp.int32(0), any_lane(walking())))

    walked = (clo_ref[...] != kf) & (chi_ref[...] == kf)

    @pl.when(any_lane(walked))
    def _():
        cand = hi_ref[...]
        cnt = count(lambda t, k0: t >= cand[None])
        lo_ref[...] = jnp.where(walked, cand, lo_ref[...])
        clo_ref[...] = jnp.where(walked, cnt, clo_ref[...])

    hi_ref[...] = jnp.where(clo_ref[...] != kf, hi_bisect, hi_ref[...])
    bisect(MAX_BISECT)

    thr = lo_ref[...]
    tied = clo_ref[...] != kf
    hi_ref[...] = jnp.full((sub, tq), float(sc_ref.shape[0]), F32)

    def key_index(k0):
        return (k0 + lax.broadcasted_iota(jnp.int32, (groups, sub, tq), 0) * sub
                + lax.broadcasted_iota(jnp.int32, (groups, sub, tq), 1)).astype(F32)

    any_tied = jnp.sum(jnp.where(tied[0:1, :], 1.0, 0.0)) > 0.0

    @pl.when(any_tied)
    def _():
        need = kf - count(lambda t, k0: t > thr[None])

        def step(_, bounds):
            below, keep = bounds
            mid = jnp.floor((below + keep) * 0.5)
            cnt = count(lambda t, k0: (t == thr[None]) & (key_index(k0) <= mid[None]))
            ok = cnt >= need
            return jnp.where(ok, below, mid), jnp.where(ok, mid, keep)

        last_key = (n_tiles * tk - 1).astype(F32)
        steps = max(1, math.ceil(math.log2(sc_ref.shape[0]))) + 1
        _, keep = lax.fori_loop(0, steps, step, (jnp.full((sub, tq), -1.0, F32),
                                                 jnp.zeros((sub, tq), F32) + last_key))
        hi_ref[...] = jnp.where(tied, keep, hi_ref[...])

    keep = hi_ref[...]

    def to_bias(with_ties):
        def tile(jt, _):
            k0 = pl.multiple_of(jt * tk, tk)
            t = sc_ref[pl.ds(k0, tk), :].reshape(groups, sub, tq)
            if with_ties:
                sel = (t > thr[None]) | ((t == thr[None]) & (key_index(k0) <= keep[None]))
            else:
                sel = t >= thr[None]
            sc_ref[pl.ds(k0, tk), :] = jnp.where(sel, 0.0, NEG_MASK).reshape(tk, tq)
            return 0
        return lambda: lax.fori_loop(0, n_tiles, tile, 0)

    lax.cond(any_tied, to_bias(True), to_bias(False))


def _dsa_kernel(qblk_ref, ktile_ref, q_ref, k_ref, vt_ref, sg_ref, qi_ref, wt_ref, kip_ref, o_ref,
                sc_ref, s_ref, p_ref, a_ref, m_ref, l_ref, acc_ref, lo_ref, hi_ref, clo_ref, chi_ref,
                *, tq, tk, ksub, top_k):
    step = pl.program_id(0)
    i = qblk_ref[step]
    j = ktile_ref[step]
    last = (i * tq + tq - 1) // tk
    sub = SUBLANES

    @pl.when(j == 0)
    def _index_and_select():
        qpos = i * tq + lax.broadcasted_iota(jnp.int32, (ksub, tq), 1)

        def score_sub(u, carry):
            rmax, rmin = carry
            k0 = pl.multiple_of(u * ksub, ksub)
            acc = jnp.zeros((ksub, tq), F32)
            lhs = kip_ref[u]
            for p in range(IDX_HEADS // 2):
                rhs = qi_ref[:, p * LANES:(p + 1) * LANES]
                s = lax.dot_general(lhs, rhs, NT_DIMS, preferred_element_type=F32)
                acc = acc + jnp.maximum(s[:ksub], 0.0) * wt_ref[2 * p:2 * p + 1, :]
                acc = acc + jnp.maximum(s[ksub:], 0.0) * wt_ref[2 * p + 1:2 * p + 2, :]
            kpos = u * ksub + lax.broadcasted_iota(jnp.int32, (ksub, tq), 0)
            adm = kpos <= qpos
            sc_ref[pl.ds(k0, ksub), :] = jnp.where(adm, acc, NEG_MASK)
            hi_part = jnp.where(adm, acc, NEG_MASK).reshape(ksub // sub, sub, tq)
            lo_part = jnp.where(adm, acc, -NEG_MASK).reshape(ksub // sub, sub, tq)
            return (jnp.maximum(rmax, _fold_groups(jnp.max, hi_part)),
                    jnp.minimum(rmin, _fold_groups(jnp.min, lo_part)))

        def score_tile(jt, carry):
            for r in range(tk // ksub):
                carry = score_sub(jt * (tk // ksub) + r, carry)
            return carry

        rmax, rmin = lax.fori_loop(
            0, last + 1, score_tile,
            (jnp.full((sub, tq), NEG_MASK, F32), jnp.full((sub, tq), -NEG_MASK, F32)))
        rmax = _sublane_all(jnp.maximum, rmax)
        rmin = _sublane_all(jnp.minimum, rmin)

        _select_to_bias(sc_ref, lo_ref, hi_ref, clo_ref, chi_ref, rmax, rmin,
                        first_query=i * tq, n_tiles=last + 1, tk=tk, top_k=top_k)

        m_ref[...] = jnp.full(m_ref.shape, NEG_MASK, F32)
        l_ref[...] = jnp.zeros(l_ref.shape, F32)
        acc_ref[...] = jnp.zeros(acc_ref.shape, F32)

    def _attend():
        kbase = pl.multiple_of(j * tk, tk)

        def logits(h, slot):
            s_ref[slot] = lax.dot_general(k_ref[h], q_ref[h], NT_DIMS, preferred_element_type=F32)

        def softmax(h, s_slot, slot):
            x = (s_ref[s_slot] + sc_ref[pl.ds(kbase, tk), :]).reshape(tk // sub, sub, tq)
            m_prev = m_ref[h]
            m_new = jnp.maximum(m_prev, _sublane_all(jnp.maximum, _fold_groups(jnp.max, x)))
            alpha = jnp.exp2(m_prev - m_new)
            p = jnp.exp2(x - m_new[None])
            p_ref[slot] = p.reshape(tk, tq).astype(BF16)
            a_ref[slot] = alpha
            m_ref[h] = m_new

        ones_rows = jnp.ones((2 * sub, tk), BF16)

        def weighted_values(h, slot):
            lhs = jnp.concatenate([vt_ref[h], ones_rows], axis=0)
            res = jnp.dot(lhs, p_ref[slot], preferred_element_type=F32)
            alpha = a_ref[slot]
            acc = acc_ref[h].reshape(HEAD_DIM // sub, sub, tq) * alpha[None]
            acc_ref[h] = acc.reshape(HEAD_DIM, tq) + res[:HEAD_DIM]
            l_ref[h] = alpha * l_ref[h] + res[HEAD_DIM:HEAD_DIM + sub]

        n_s, n_p = s_ref.shape[0], p_ref.shape[0]
        ahead_s = n_s - 1
        lag = ahead_s + n_p - 1
        for t in range(N_HEADS + lag):
            if t < N_HEADS:
                logits(t, t % n_s)
            hs_ = t - ahead_s
            if 0 <= hs_ < N_HEADS:
                softmax(hs_, hs_ % n_s, hs_ % n_p)
            hv = t - lag
            if 0 <= hv < N_HEADS:
                weighted_values(hv, hv % n_p)

    _attend()

    @pl.when(j == last)
    def _finish():
        for h in range(N_HEADS):
            hs = slice(h * HEAD_DIM, (h + 1) * HEAD_DIM)
            l = l_ref[h]
            o_t = (acc_ref[h].reshape(HEAD_DIM // sub, sub, tq) / l[None]).reshape(HEAD_DIM, tq)
            o_ref[:, hs] = (o_t.T * sg_ref[:, hs].astype(F32)).astype(o_ref.dtype)


def _dsa_attention(qk, vt, sg, qi, wt, kip, *, tq=256, tk=512, ksub=128):
    s = sg.shape[0]
    tq = min(tq, s)
    tk = min(tk, s)
    top_k = min(TOP_K_MAX, s // 4)
    aw = ATTN_WIDTH

    pairs = [(i, j) for i in range(s // tq) for j in range((i * tq + tq - 1) // tk + 1)]
    qblk = jnp.asarray([p[0] for p in pairs], jnp.int32)
    ktile = jnp.asarray([p[1] for p in pairs], jnp.int32)

    grid_spec = pltpu.PrefetchScalarGridSpec(
        num_scalar_prefetch=2,
        grid=(len(pairs),),
        in_specs=[
            pl.BlockSpec((N_HEADS, tq, HEAD_DIM), lambda t, qb, kt: (0, qb[t], 0)),
            pl.BlockSpec((N_HEADS, tk, HEAD_DIM), lambda t, qb, kt: (1, kt[t], 0)),
            pl.BlockSpec((N_HEADS, HEAD_DIM, tk), lambda t, qb, kt: (0, 0, kt[t])),
            pl.BlockSpec((tq, aw), lambda t, qb, kt: (qb[t], 0)),
            pl.BlockSpec((tq, IDX_HEADS * IDX_DIM), lambda t, qb, kt: (qb[t], 0)),
            pl.BlockSpec((IDX_HEADS, tq), lambda t, qb, kt: (0, qb[t])),
            pl.BlockSpec(kip.shape, lambda t, qb, kt: (0, 0, 0)),
        ],
        out_specs=pl.BlockSpec((tq, aw), lambda t, qb, kt: (qb[t], 0)),
        scratch_shapes=[
            pltpu.VMEM((s, tq), F32),
            pltpu.VMEM((LOGITS_RING, tk, tq), F32),
            pltpu.VMEM((PROBS_RING, tk, tq), BF16),
            pltpu.VMEM((PROBS_RING, SUBLANES, tq), F32),
            pltpu.VMEM((N_HEADS, SUBLANES, tq), F32),
            pltpu.VMEM((N_HEADS, SUBLANES, tq), F32),
            pltpu.VMEM((N_HEADS, HEAD_DIM, tq), F32),
            pltpu.VMEM((SUBLANES, tq), F32),
            pltpu.VMEM((SUBLANES, tq), F32),
            pltpu.VMEM((SUBLANES, tq), F32),
            pltpu.VMEM((SUBLANES, tq), F32),
        ],
    )
    return pl.pallas_call(
        functools.partial(_dsa_kernel, tq=tq, tk=tk, ksub=ksub, top_k=top_k),
        grid_spec=grid_spec,
        out_shape=jax.ShapeDtypeStruct((s, aw), BF16),
        compiler_params=_params(("arbitrary",)),
        name="dsa_index_select_attend",
    )(qblk, ktile, qk, qk, vt, sg, qi, wt, kip)


def _pool_mixer(h, w_in_stack, layer, w_grp, layer_scale):
    d = D_MODEL
    v = _mm(h, w_in_stack, layer, 0, d, epilogue="plain", out_dtype=F32, name="pool_in_value")
    sgate = _mm(h, w_in_stack, layer, d, d, epilogue="silu", out_dtype=BF16, name="pool_in_gate")
    return _pool_core(v, sgate, w_grp.astype(BF16), layer_scale.reshape(1, d))


def _dsa_mixer(h, w_in_stack, layer, tabs):
    aw = ATTN_WIDTH
    nqi = IDX_HEADS * IDX_DIM
    w_in = w_in_stack[layer]
    w_vt = w_in[:, 2 * aw:3 * aw].T.astype(BF16)
    n_idx = nqi + LANES
    w_idx = jnp.pad(w_in[:, 4 * aw:], ((0, 0), (0, n_idx - (w_in.shape[1] - 4 * aw)))).astype(BF16)

    tn = 1024
    qk = _mm_rope_heads(h, w_in_stack, layer, 0, 2 * aw, tabs, tab_base=0, tab_split=aw // tn,
                        half=ROT_DIM // 2, tn=tn, name="dsa_in_qk")
    vt = _mm_heads_t(w_vt, h, tm=1024, name="dsa_in_vt")
    sg = _mm(h, w_in_stack, layer, 3 * aw, aw, epilogue="silu", out_dtype=BF16, name="dsa_in_gate")
    qi, kw = _mm_idx(h, w_idx, tabs, tab_pair=1, half=IDX_ROT // 2)

    ki = kw[:, :IDX_DIM].astype(BF16)
    zeros = jnp.zeros_like(ki)
    ksub = 128
    nsub = ki.shape[0] // ksub
    kip = jnp.concatenate([jnp.concatenate([ki, zeros], axis=1).reshape(nsub, ksub, LANES),
                           jnp.concatenate([zeros, ki], axis=1).reshape(nsub, ksub, LANES)],
                          axis=1)
    wt = kw[:, IDX_DIM:IDX_DIM + IDX_HEADS].T
    return _dsa_attention(qk, vt, sg, qi, wt, kip, ksub=ksub)


@jax.jit
def kernel(x, c, positions, norm_g, mod_w, mod_b, pool_w_in, pool_w_grp, pool_scale, pool_w_out,
           attn_w_in, attn_w_out, final_g):
    b, s, d = x.shape
    assert b == 1 and d == D_MODEL
    xs = x[0]
    mod = _modulation(c, mod_w, mod_b)

    attn_scale = HEAD_DIM ** -0.5
    idx_scale = (IDX_HEADS ** -0.5) * (IDX_DIM ** -0.5)
    q_fold = attn_scale * math.log2(math.e)
    consts = jnp.stack([
        _rope_consts(HEAD_DIM, ROT_DIM, q_fold, LANES, 1.0, 0, 0),
        _rope_consts(HEAD_DIM, ROT_DIM, 1.0, LANES, 1.0, 0, 0),
        _rope_consts(IDX_DIM, IDX_ROT, 1.0, LANES, 1.0, 0, 0),
        _rope_consts(IDX_DIM, IDX_ROT, 1.0, IDX_DIM, idx_scale, IDX_DIM, IDX_DIM + IDX_HEADS),
    ])
    tabs = _rope_tables(positions[0], consts)

    zero_row = jnp.zeros((1, d), F32)

    def mod_rows(i):
        return mod[i, :, :d], mod[i, :, d:2 * d], mod[i, :, 2 * d:]

    shift, scale, gate = mod_rows(0)
    h = _norm(xs, norm_g[0].reshape(1, d), scale, shift, modulate=True, out_dtype=BF16)
    for i in range(DEPTH):
        jl = i // 2
        if i % 2 == 0:
            branch = _pool_mixer(h, pool_w_in, jl, pool_w_grp[jl], pool_scale[jl])
            w_out = pool_w_out[jl]
        else:
            branch = _dsa_mixer(h, attn_w_in, jl, tabs)
            w_out = attn_w_out[jl]
        if i + 1 < DEPTH:
            shift, scale, next_gate = mod_rows(i + 1)
            xs, h = _mm_resid_norm(branch, w_out.astype(BF16), xs, gate, norm_g[i + 1].reshape(1, d),
                                   scale, shift, modulate=True, emit_x=True, out_dtype=BF16,
                                   name="out_residual_norm")
            gate = next_gate
        else:
            out = _mm_resid_norm(branch, w_out.astype(BF16), xs, gate, final_g.reshape(1, d),
                                 zero_row, zero_row, modulate=False, emit_x=False, out_dtype=F32,
                                 name="out_residual_final_norm")
    return out[None]
```
